```python
import math
import jax, jax.numpy as jnp
from jax import lax
import numpy as np

D_MODEL = 1024
BATCH = 2
SEQ = 8192
DEPTH = 1

GRID_W = 64
CTX_LEN = 256
EPS = 1e-6
N_MOD = 6
SSD_EXPAND = 2
D_SSD = SSD_EXPAND * D_MODEL
SSD_HEADDIM = 64
SSD_HEADS = D_SSD // SSD_HEADDIM
SSD_GROUPS = 4
SSD_HPG = SSD_HEADS // SSD_GROUPS
SSD_STATE = 128
SSD_CONV = 3
SSD_CHUNK = 128
D_BC = SSD_GROUPS * SSD_STATE
D_XBC = D_SSD + 2 * D_BC
DT_MIN = 1e-3
DT_MAX = 1e-1
ATT_HEADS = 16
ATT_KV_HEADS = 4
ATT_GROUP = ATT_HEADS // ATT_KV_HEADS
ATT_HEADDIM = 64
WINDOW = 128
ATT_BLOCK = 128
D_Q = ATT_HEADS * ATT_HEADDIM
D_KV = ATT_KV_HEADS * ATT_HEADDIM
ROPE_BASE = 10000.0
D_FF = 2816
FFN_CONV = 3
N_BRANCH = 2
Z0 = 0
XBC0 = Z0 + D_SSD
DT0 = XBC0 + D_XBC
Q0 = DT0 + 2 * SSD_HEADS
K0 = Q0 + D_Q
V0 = K0 + D_KV
G0 = V0 + D_KV
D_IN = G0 + N_BRANCH * D_MODEL

kernel_name = 'hybrid_ssd_swa_convffn_dit_block'


def rmsnorm(x, g):
    x32 = x.astype(jnp.float32)
    y = x32 * lax.rsqrt(jnp.mean(x32 * x32, axis=-1, keepdims=True) + EPS)
    return (y * g.astype(jnp.float32)).astype(x.dtype)


def modulate(x, g, shift, scale):
    return rmsnorm(x, g) * (1 + scale) + shift


def dwconv_centred(x, w, b):
    K, C = w.shape
    y = lax.conv_general_dilated(x, w[:, None, :].astype(x.dtype), window_strides=(1,),
                                 padding=[(K // 2, K // 2)],
                                 dimension_numbers=('NWC', 'WIO', 'NWC'),
                                 feature_group_count=C)
    return y + b.astype(x.dtype)


def segsum(a):
    T = a.shape[-1]
    cs = jnp.cumsum(a, axis=-1)
    diff = cs[..., :, None] - cs[..., None, :]
    return jnp.where(jnp.tril(jnp.ones((T, T), dtype=bool)), diff, -jnp.inf)


def ssd_scan(xs, la, bm, cm, h0, with_output):
    f32 = jnp.float32
    Bsz, L, G, R, P = xs.shape
    N = bm.shape[-1]
    T = SSD_CHUNK
    nc = L // T
    xc = xs.astype(f32).reshape(Bsz, nc, T, G, R, P)
    bc = bm.astype(f32).reshape(Bsz, nc, T, G, N)
    cc = cm.astype(f32).reshape(Bsz, nc, T, G, N)
    ac = la.astype(f32).reshape(Bsz, nc, T, G, R).transpose(0, 3, 4, 1, 2)
    cs = jnp.cumsum(ac, axis=-1)
    decay_to_end = jnp.exp(cs[..., -1:] - cs)
    states = jnp.einsum('bcsgn,bgrcs,bcsgrp->bcgrpn', bc, decay_to_end, xc)
    if h0 is None:
        h0 = jnp.zeros((Bsz, G, R, P, N), f32)
    states = jnp.concatenate([h0[:, None].astype(f32), states], axis=1)
    chunk_decay = jnp.exp(segsum(jnp.pad(cs[..., -1], ((0, 0), (0, 0), (0, 0), (1, 0)))))
    states = jnp.einsum('bgrzc,bcgrpn->bzgrpn', chunk_decay, states)
    final = states[:, -1]
    if not with_output:
        return None, final
    cb = jnp.einsum('bclgn,bcsgn->bgcls', cc, bc)
    scores = cb[:, :, None] * jnp.exp(segsum(ac))
    y_diag = jnp.einsum('bgrcls,bcsgrp->bclgrp', scores, xc)
    y_off = jnp.einsum('bclgn,bcgrpn,bgrcl->bclgrp', cc, states[:, :-1], jnp.exp(cs))
    return (y_diag + y_off).reshape(Bsz, L, G, R, P).astype(xs.dtype), final


def ssd_prepare(xbc_raw, dt_raw, conv_w, conv_b, dt_bias, a_log):
    xbc = jax.nn.silu(dwconv_centred(xbc_raw, conv_w, conv_b))
    Bsz, L, _ = xbc.shape
    xs = xbc[..., :D_SSD].reshape(Bsz, L, SSD_GROUPS, SSD_HPG, SSD_HEADDIM)
    bm = xbc[..., D_SSD:D_SSD + D_BC].reshape(Bsz, L, SSD_GROUPS, SSD_STATE)
    cm = xbc[..., D_SSD + D_BC:].reshape(Bsz, L, SSD_GROUPS, SSD_STATE)
    dt = jax.nn.softplus(dt_raw.astype(jnp.float32).reshape(Bsz, L, 2, SSD_HEADS)
                         + dt_bias.astype(jnp.float32))
    la = dt * (-jnp.exp(a_log.astype(jnp.float32)))
    dt = dt.reshape(Bsz, L, 2, SSD_GROUPS, SSD_HPG)
    la = la.reshape(Bsz, L, 2, SSD_GROUPS, SSD_HPG)
    return xs, bm, cm, dt, la


def ssd_bidir(xs, bm, cm, dt, la, h0f, h0b, with_output):
    xf = xs * dt[:, :, 0, :, :, None].astype(xs.dtype)
    xb = xs * dt[:, :, 1, :, :, None].astype(xs.dtype)
    yf, hf = ssd_scan(xf, la[:, :, 0], bm, cm, h0f, with_output)
    yb, hb = ssd_scan(jnp.flip(xb, axis=1), jnp.flip(la[:, :, 1], axis=1), jnp.flip(bm, axis=1),
                      jnp.flip(cm, axis=1), h0b, with_output)
    y = yf + jnp.flip(yb, axis=1) if with_output else None
    return y, hf, hb


def ssd_output(y, xs, z, d_skip, g):
    Bsz, L = y.shape[:2]
    y = y + d_skip.reshape(SSD_GROUPS, SSD_HPG, 1).astype(y.dtype) * xs
    return rmsnorm(y.reshape(Bsz, L, D_SSD) * jax.nn.silu(z), g)


def axial_rope(t, row, col):
    half = ATT_HEADDIM // 2
    inv = ROPE_BASE ** (-jnp.arange(0, half, 2, dtype=jnp.float32) / half)

    def rot(u, pos):
        ang = pos.astype(jnp.float32)[:, None] * inv[None]
        cos = jnp.cos(ang)[None, :, None, :].astype(u.dtype)
        sin = jnp.sin(ang)[None, :, None, :].astype(u.dtype)
        u1, u2 = u[..., :half // 2], u[..., half // 2:]
        return jnp.concatenate([u1 * cos - u2 * sin, u2 * cos + u1 * sin], axis=-1)

    return jnp.concatenate([rot(t[..., :half], row), rot(t[..., half:], col)], axis=-1)


def window_ctx_attention(q, k, v, kc, vc, sink):
    Bsz, L, H, Dh = q.shape
    nb = L // ATT_BLOCK
    scale = Dh ** -0.5
    qb = q.reshape(Bsz, nb, ATT_BLOCK, ATT_KV_HEADS, ATT_GROUP, Dh)

    def band(t):
        tb = t.reshape(Bsz, nb, ATT_BLOCK, ATT_KV_HEADS, Dh)
        tp = jnp.pad(tb, ((0, 0), (1, 1), (0, 0), (0, 0), (0, 0)))
        return jnp.concatenate([tp[:, :-2], tp[:, 1:-1], tp[:, 2:]], axis=2)

    kb, vb = band(k), band(v)
    s_win = jnp.einsum('bnqkgd,bnjkd->bnkgqj', qb, kb) * scale
    s_ctx = jnp.einsum('bnqkgd,bckd->bnkgqc', qb, kc) * scale
    qpos = jnp.arange(nb)[:, None] * ATT_BLOCK + jnp.arange(ATT_BLOCK)[None]
    kpos = (jnp.arange(nb)[:, None] - 1) * ATT_BLOCK + jnp.arange(3 * ATT_BLOCK)[None]
    valid = ((jnp.abs(qpos[:, :, None] - kpos[:, None, :]) <= WINDOW)
             & (kpos >= 0)[:, None, :] & (kpos < L)[:, None, :])
    s_win = jnp.where(valid[None, :, None, None], s_win, -jnp.inf)
    s_sink = jnp.broadcast_to(sink.reshape(1, 1, ATT_KV_HEADS, ATT_GROUP, 1, 1).astype(s_win.dtype),
                              s_win.shape[:-1] + (1,))
    logits = jnp.concatenate([s_win, s_ctx, s_sink], axis=-1).astype(jnp.float32)
    p = jax.nn.softmax(logits, axis=-1).astype(v.dtype)
    nw = 3 * ATT_BLOCK
    Lc = kc.shape[1]
    o = (jnp.einsum('bnkgqj,bnjkd->bnqkgd', p[..., :nw], vb)
         + jnp.einsum('bnkgqc,bckd->bnqkgd', p[..., nw:nw + Lc], vc))
    return o.reshape(Bsz, L, H * Dh)


def ctx_self_attention(q, k, v, sink):
    Bsz, Lc, H, Dh = q.shape
    qg = q.reshape(Bsz, Lc, ATT_KV_HEADS, ATT_GROUP, Dh)
    s = jnp.einsum('bqkgd,bckd->bkgqc', qg, k) * (Dh ** -0.5)
    s_sink = jnp.broadcast_to(sink.reshape(1, ATT_KV_HEADS, ATT_GROUP, 1, 1).astype(s.dtype),
                              s.shape[:-1] + (1,))
    p = jax.nn.softmax(jnp.concatenate([s, s_sink], axis=-1).astype(jnp.float32), axis=-1)
    o = jnp.einsum('bkgqc,bckd->bqkgd', p[..., :Lc].astype(v.dtype), v)
    return o.reshape(Bsz, Lc, H * Dh)


def merge_branches(p, y_ssd, y_att, w_o_ssd, w_o_att, w_out):
    gates = jax.nn.sigmoid(p[..., G0:D_IN])
    g_ssd, g_att = gates[..., :D_MODEL], gates[..., D_MODEL:]
    return (g_ssd * (y_ssd @ w_o_ssd) + g_att * (y_att @ w_o_att)) @ w_out


def token_mixer(h_lat, h_ctx, row, col, w_in, conv_w, conv_b, dt_bias, a_log, d_skip, ssd_g,
                w_o_ssd, w_o_att, sink, w_out, with_ctx_out):
    Bsz, L, _ = h_lat.shape
    Lc = h_ctx.shape[1]
    p_lat = h_lat @ w_in
    p_ctx = h_ctx @ w_in
    xs_c, bm_c, cm_c, dt_c, la_c = ssd_prepare(p_ctx[..., XBC0:DT0], p_ctx[..., DT0:Q0],
                                               conv_w, conv_b, dt_bias, a_log)
    y_c, hf_c, hb_c = ssd_bidir(xs_c, bm_c, cm_c, dt_c, la_c, None, None, with_ctx_out)
    xs_l, bm_l, cm_l, dt_l, la_l = ssd_prepare(p_lat[..., XBC0:DT0], p_lat[..., DT0:Q0],
                                               conv_w, conv_b, dt_bias, a_log)
    y_l, _, _ = ssd_bidir(xs_l, bm_l, cm_l, dt_l, la_l, hf_c, hb_c, True)
    ssd_lat = ssd_output(y_l, xs_l, p_lat[..., Z0:XBC0], d_skip, ssd_g)
    q_l = axial_rope(p_lat[..., Q0:K0].reshape(Bsz, L, ATT_HEADS, ATT_HEADDIM), row, col)
    k_l = axial_rope(p_lat[..., K0:V0].reshape(Bsz, L, ATT_KV_HEADS, ATT_HEADDIM), row, col)
    v_l = p_lat[..., V0:G0].reshape(Bsz, L, ATT_KV_HEADS, ATT_HEADDIM)
    k_c = p_ctx[..., K0:V0].reshape(Bsz, Lc, ATT_KV_HEADS, ATT_HEADDIM)
    v_c = p_ctx[..., V0:G0].reshape(Bsz, Lc, ATT_KV_HEADS, ATT_HEADDIM)
    att_lat = window_ctx_attention(q_l, k_l, v_l, k_c, v_c, sink)
    out_lat = merge_branches(p_lat, ssd_lat, att_lat, w_o_ssd, w_o_att, w_out)
    if not with_ctx_out:
        return out_lat, None
    ssd_ctx = ssd_output(y_c, xs_c, p_ctx[..., Z0:XBC0], d_skip, ssd_g)
    q_c = p_ctx[..., Q0:K0].reshape(Bsz, Lc, ATT_HEADS, ATT_HEADDIM)
    att_ctx = ctx_self_attention(q_c, k_c, v_c, sink)
    out_ctx = merge_branches(p_ctx, ssd_ctx, att_ctx, w_o_ssd, w_o_att, w_out)
    return out_lat, out_ctx


def conv_ffn(h, w_up, conv_w, conv_b, w_down):
    u = dwconv_centred(h @ w_up, conv_w, conv_b)
    a, b = u[..., :D_FF], u[..., D_FF:]
    return (jax.nn.silu(a) * b) @ w_down


def setup_inputs(seed: int = 0) -> dict:
    key = jax.random.key(seed)
    ks = jax.random.split(key, 24)
    f32 = jnp.float32

    def nrm(k, shape, scale):
        return jax.random.normal(k, shape, f32) * scale

    dt0 = jnp.exp(jax.random.uniform(ks[10], (DEPTH, 2, SSD_HEADS), f32,
                                     math.log(DT_MIN), math.log(DT_MAX)))
    return {
        'x': nrm(ks[0], (BATCH, SEQ, D_MODEL), 1.0),
        'c': nrm(ks[1], (BATCH, D_MODEL), 1.0),
        'ctx': nrm(ks[2], (BATCH, CTX_LEN, D_MODEL), 1.0),
        'c_ctx': nrm(ks[3], (D_MODEL,), 1.0),
        'w_mod': nrm(ks[4], (DEPTH, D_MODEL, N_MOD * D_MODEL), 0.5 * D_MODEL ** -0.5),
        'b_mod': nrm(ks[5], (DEPTH, N_MOD * D_MODEL), 0.01),
        'norm1_g': 1.0 + nrm(ks[6], (DEPTH, D_MODEL), 0.05),
        'norm2_g': 1.0 + nrm(ks[7], (DEPTH, D_MODEL), 0.05),
        'w_in': nrm(ks[8], (DEPTH, D_MODEL, D_IN), D_MODEL ** -0.5),
        'ssd_conv_w': nrm(ks[9], (DEPTH, SSD_CONV, D_XBC), SSD_CONV ** -0.5),
        'ssd_conv_b': nrm(ks[11], (DEPTH, D_XBC), 0.01),
        'ssd_dt_bias': dt0 + jnp.log(-jnp.expm1(-dt0)),
        'ssd_a_log': jnp.log(jax.random.uniform(ks[12], (DEPTH, 2, SSD_HEADS), f32, 1.0, 16.0)),
        'ssd_d': 1.0 + nrm(ks[13], (DEPTH, SSD_HEADS), 0.1),
        'ssd_norm_g': 1.0 + nrm(ks[14], (DEPTH, D_SSD), 0.05),
        'w_o_ssd': nrm(ks[15], (DEPTH, D_SSD, D_MODEL), D_SSD ** -0.5),
        'w_o_att': nrm(ks[16], (DEPTH, D_Q, D_MODEL), D_Q ** -0.5),
        'att_sink': nrm(ks[17], (DEPTH, ATT_HEADS), 0.5),
        'w_out': nrm(ks[18], (DEPTH, D_MODEL, D_MODEL), D_MODEL ** -0.5),
        'w_up': nrm(ks[19], (DEPTH, D_MODEL, 2 * D_FF), D_MODEL ** -0.5),
        'ffn_conv_w': nrm(ks[20], (DEPTH, FFN_CONV, 2 * D_FF), FFN_CONV ** -0.5),
        'ffn_conv_b': nrm(ks[21], (DEPTH, 2 * D_FF), 0.01),
        'w_down': nrm(ks[22], (DEPTH, D_FF, D_MODEL), D_FF ** -0.5),
        'final_g': 1.0 + nrm(ks[23], (D_MODEL,), 0.05),
    }


def reference(x, c, ctx, c_ctx, w_mod, b_mod, norm1_g, norm2_g, w_in, ssd_conv_w, ssd_conv_b,
              ssd_dt_bias, ssd_a_log, ssd_d, ssd_norm_g, w_o_ssd, w_o_att, att_sink, w_out,
              w_up, ffn_conv_w, ffn_conv_b, w_down, final_g):
    L = x.shape[1]
    ROWS = L // GRID_W
    row = jnp.broadcast_to(jnp.arange(ROWS)[:, None], (ROWS, GRID_W)).reshape(-1)
    col = jnp.broadcast_to(jnp.arange(GRID_W)[None, :], (ROWS, GRID_W)).reshape(-1)
    for l in range(DEPTH):
        last = l == DEPTH - 1
        mod = jax.nn.silu(c) @ w_mod[l] + b_mod[l]
        sh1, sc1, gt1, sh2, sc2, gt2 = jnp.split(mod[:, None, :], N_MOD, axis=-1)
        mod_c = jax.nn.silu(c_ctx) @ w_mod[l] + b_mod[l]
        csh1, csc1, cgt1, csh2, csc2, cgt2 = jnp.split(mod_c, N_MOD, axis=-1)
        h_lat = modulate(x, norm1_g[l], sh1, sc1)
        h_ctx = modulate(ctx, norm1_g[l], csh1, csc1)
        o_lat, o_ctx = token_mixer(h_lat, h_ctx, row, col, w_in[l], ssd_conv_w[l], ssd_conv_b[l],
                                   ssd_dt_bias[l], ssd_a_log[l], ssd_d[l], ssd_norm_g[l],
                                   w_o_ssd[l], w_o_att[l], att_sink[l], w_out[l], not last)
        x = x + gt1 * o_lat
        x = x + gt2 * conv_ffn(modulate(x, norm2_g[l], sh2, sc2), w_up[l], ffn_conv_w[l],
                               ffn_conv_b[l], w_down[l])
        if not last:
            ctx = ctx + cgt1 * o_ctx
            ctx = ctx + cgt2 * conv_ffn(modulate(ctx, norm2_g[l], csh2, csc2), w_up[l],
                                        ffn_conv_w[l], ffn_conv_b[l], w_down[l])
    return rmsnorm(x, final_g)
```

```python
import functools
import math

import jax
import jax.numpy as jnp
import numpy as np
from jax import lax
from jax.experimental import pallas as pl
from jax.experimental.pallas import tpu as pltpu

F32 = jnp.float32
BF16 = jnp.bfloat16

D_MODEL = 1024
GRID_W = 64
EPS = 1e-6
N_MOD = 6
D_SSD = 2048
SSD_HEADDIM = 64
SSD_HEADS = 32
SSD_GROUPS = 4
SSD_HPG = 8
SSD_STATE = 128
SSD_CHUNK = 128
D_BC = SSD_GROUPS * SSD_STATE
D_XBC = D_SSD + 2 * D_BC
ATT_HEADS = 16
ATT_KV_HEADS = 4
ATT_GROUP = 4
ATT_HEADDIM = 64
ATT_BLOCK = 128
D_Q = ATT_HEADS * ATT_HEADDIM
D_KV = ATT_KV_HEADS * ATT_HEADDIM
ROPE_BASE = 10000.0
D_FF = 2816
Z0 = 0
XBC0 = Z0 + D_SSD
DT0 = XBC0 + D_XBC
Q0 = DT0 + 2 * SSD_HEADS
K0 = Q0 + D_Q
V0 = K0 + D_KV
G0 = V0 + D_KV
D_IN = G0 + 2 * D_MODEL

V7X_VMEM_BYTES = 64 * 1024 * 1024
LANES = 128
SUBLANES = 8
HALO = SUBLANES
NEG_BIG = -1e30


def _vmem_limit(nbytes):
    return int(min(nbytes, V7X_VMEM_BYTES - 8 * 1024 * 1024))


def _resident(shape):
    nd = len(shape)
    return pl.BlockSpec(shape, lambda *_: (0,) * nd, pipeline_mode=pl.Buffered(1))


def _sigmoid(x):
    return 1.0 / (1.0 + jnp.exp(-x))


def _silu(x):
    return x * _sigmoid(x)


def _dot(a, b):
    return jnp.dot(a, b, preferred_element_type=F32)


def _dot_nt(a, b):
    return lax.dot_general(a, b, (((1,), (1,)), ((), ())), preferred_element_type=F32)


def _modulated_norm(x, g, shift, scale):
    ms = jnp.mean(x * x, axis=-1, keepdims=True)
    return x * lax.rsqrt(ms + EPS) * g * (1.0 + scale) + shift


def _mod_kernel(c_ref, w_ref, b_ref, o_ref):
    s = _silu(c_ref[...])
    o_ref[...] = _dot(s.astype(BF16), w_ref[...].astype(BF16)) + b_ref[...]


def _mod_call(cc, w_mod, b_mod):
    n = w_mod.shape[1]
    tn = 1536
    return pl.pallas_call(
        _mod_kernel,
        grid=(n // tn,),
        in_specs=[pl.BlockSpec((SUBLANES, D_MODEL), lambda j: (0, 0)),
                  pl.BlockSpec((D_MODEL, tn), lambda j: (0, j)),
                  pl.BlockSpec((1, tn), lambda j: (0, j))],
        out_specs=pl.BlockSpec((SUBLANES, tn), lambda j: (0, j)),
        out_shape=jax.ShapeDtypeStruct((SUBLANES, n), F32),
        compiler_params=pltpu.CompilerParams(dimension_semantics=("arbitrary",),
                                             vmem_limit_bytes=_vmem_limit(40 << 20)),
        name="mod",
    )(cc, w_mod, b_mod)


def _halo_maps(tm, n_rows):
    hb = tm // HALO
    last = n_rows // HALO - 1
    prev_map = lambda b, i: (b, jnp.maximum(i * hb - 1, 0), 0)
    next_map = lambda b, i: (b, jnp.minimum((i + 1) * hb, last), 0)
    return prev_map, next_map


def _conv3_silu_from_scratch(u_scr, tm, cw_ref, cb_ref, lo, hi):
    w = cw_ref[:, lo:hi]
    y = (u_scr[pl.ds(HALO - 1, tm), lo:hi] * w[0:1]
         + u_scr[pl.ds(HALO, tm), lo:hi] * w[1:2]
         + u_scr[pl.ds(HALO + 1, tm), lo:hi] * w[2:3])
    return y + cb_ref[:, lo:hi]


def _rope(t, cos, sin_signed, n_tiles):
    width = t.shape[1]
    lane = lax.broadcasted_iota(jnp.int32, t.shape, 1)
    first = (lane % 32) < 16
    partner = jnp.where(first, pltpu.roll(t, width - 16, 1), pltpu.roll(t, 16, 1))
    cos_t = jnp.concatenate([cos] * n_tiles, axis=1)
    sin_t = jnp.concatenate([sin_signed] * n_tiles, axis=1)
    return t * cos_t + partner * sin_t


def _inproj_kernel(*refs, tm, latent):
    if latent:
        (x_ref, xp_ref, xn_ref, mod_ref, g_ref, wxbc_ref, wdt_ref, cw_ref, cb_ref, dtb_ref, am_ref,
         wk_ref, wv_ref, wz_ref, wq_ref, wg_ref, cos_ref, sin_ref,
         xbc_o, dtla_o, k_o, v_o, z_o, q_o, gate_o, u_scr) = refs
    else:
        (x_ref, xp_ref, xn_ref, mod_ref, g_ref, wxbc_ref, wdt_ref, cw_ref, cb_ref, dtb_ref, am_ref,
         wk_ref, wv_ref,
         xbc_o, dtla_o, k_o, v_o, u_scr) = refs
    i = pl.program_id(1)
    n_i = pl.num_programs(1)
    g = g_ref[...]
    shift = mod_ref[0, :, 0:D_MODEL]
    scale = mod_ref[0, :, D_MODEL:2 * D_MODEL]
    h = _modulated_norm(x_ref[0], g, shift, scale)
    hp = _modulated_norm(xp_ref[0], g, shift, scale) * (i > 0).astype(F32)
    hn = _modulated_norm(xn_ref[0], g, shift, scale) * (i < n_i - 1).astype(F32)
    hb = h.astype(BF16)
    h_ext = jnp.concatenate([hp, h, hn], axis=0).astype(BF16)

    u_scr[...] = _dot(h_ext, wxbc_ref[...])
    cchunk = 512
    for c0 in range(0, D_XBC, cchunk):
        y = _conv3_silu_from_scratch(u_scr, tm, cw_ref, cb_ref, c0, c0 + cchunk)
        xbc_o[0, :, c0:c0 + cchunk] = _silu(y).astype(BF16)

    raw = _dot(hb, wdt_ref[...]) + dtb_ref[...]
    dt = jnp.maximum(raw, 0.0) + jnp.log1p(jnp.exp(-jnp.abs(raw)))
    lane = lax.broadcasted_iota(jnp.int32, dt.shape, 1)
    dtla_o[0] = jnp.where(lane < 2 * SSD_HEADS, dt, dt * am_ref[...])

    kf = _dot(hb, wk_ref[...])
    if latent:
        cos = cos_ref[...]
        sin = sin_ref[...]
        kf = _rope(kf, cos, sin, D_KV // LANES)
    k_o[0] = kf.astype(BF16)
    v_o[0] = _dot(hb, wv_ref[...]).astype(BF16)
    if latent:
        z_o[0] = _silu(_dot(hb, wz_ref[...])).astype(BF16)
        qf = _rope(_dot(hb, wq_ref[...]), cos, sin, D_Q // LANES)
        q_o[0] = (qf * (ATT_HEADDIM ** -0.5)).astype(BF16)
        gate_o[0] = _sigmoid(_dot(hb, wg_ref[...])).astype(BF16)


def _inproj_call(x, modv, per_batch_mod, norm_g, w, consts, rope, tm, latent):
    bsz, n_rows, _ = x.shape
    prev_map, next_map = _halo_maps(tm, n_rows)
    row_map = lambda b, i: (b, i, 0)
    mod_map = (lambda b, i: (b, 0, 0)) if per_batch_mod else (lambda b, i: (0, 0, 0))
    in_specs = [pl.BlockSpec((1, tm, D_MODEL), row_map),
                pl.BlockSpec((1, HALO, D_MODEL), prev_map),
                pl.BlockSpec((1, HALO, D_MODEL), next_map),
                pl.BlockSpec((1, 1, N_MOD * D_MODEL), mod_map),
                _resident((1, D_MODEL)),
                _resident((D_MODEL, D_XBC)), _resident((D_MODEL, LANES)),
                _resident((3, D_XBC)), _resident((1, D_XBC)),
                _resident((1, LANES)), _resident((1, LANES)),
                _resident((D_MODEL, D_KV)), _resident((D_MODEL, D_KV))]
    args = [x, x, x, modv, norm_g, w["xbc"], w["dt"], consts["conv_w"], consts["conv_b"],
            consts["dt_bias"], consts["a_mult"], w["k"], w["v"]]
    out_shape = [jax.ShapeDtypeStruct((bsz, n_rows, D_XBC), BF16),
                 jax.ShapeDtypeStruct((bsz, n_rows, LANES), F32),
                 jax.ShapeDtypeStruct((bsz, n_rows, D_KV), BF16),
                 jax.ShapeDtypeStruct((bsz, n_rows, D_KV), BF16)]
    out_specs = [pl.BlockSpec((1, tm, D_XBC), row_map), pl.BlockSpec((1, tm, LANES), row_map),
                 pl.BlockSpec((1, tm, D_KV), row_map), pl.BlockSpec((1, tm, D_KV), row_map)]
    if latent:
        in_specs += [_resident((D_MODEL, D_SSD)), _resident((D_MODEL, D_Q)),
                     _resident((D_MODEL, 2 * D_MODEL)),
                     pl.BlockSpec((tm, LANES), lambda b, i: (i, 0)),
                     pl.BlockSpec((tm, LANES), lambda b, i: (i, 0))]
        args += [w["z"], w["q"], w["g"], rope[0], rope[1]]
        out_shape += [jax.ShapeDtypeStruct((bsz, n_rows, D_SSD), BF16),
                      jax.ShapeDtypeStruct((bsz, n_rows, D_Q), BF16),
                      jax.ShapeDtypeStruct((bsz, n_rows, 2 * D_MODEL), BF16)]
        out_specs += [pl.BlockSpec((1, tm, D_SSD), row_map), pl.BlockSpec((1, tm, D_Q), row_map),
                      pl.BlockSpec((1, tm, 2 * D_MODEL), row_map)]
    return pl.pallas_call(
        functools.partial(_inproj_kernel, tm=tm, latent=latent),
        grid=(bsz, n_rows // tm),
        in_specs=in_specs, out_specs=out_specs, out_shape=out_shape,
        scratch_shapes=[pltpu.VMEM((tm + 2 * HALO, D_XBC), F32)],
        compiler_params=pltpu.CompilerParams(dimension_semantics=("arbitrary", "arbitrary"),
                                             vmem_limit_bytes=_vmem_limit(56 << 20)),
        name="inproj_lat" if latent else "inproj_ctx",
    )(*args)


QUAD = 4
QW = QUAD * SSD_HEADDIM


def _ssd_kernel(*refs, reverse, has_h0, with_y, fuse_out):
    T = SSD_CHUNK
    it = iter(refs)
    xbc_ref = next(it)
    dtla_ref = next(it)
    h0_ref = next(it) if has_h0 else None
    if fuse_out:
        yf_ref, z_ref, dskip_ref, ng_ref = next(it), next(it), next(it), next(it)
    y_o = next(it) if with_y else None
    hfin_o = next(it)
    s_scr = next(it)
    y_scr = next(it) if with_y else None

    i = pl.program_id(1)
    n_i = pl.num_programs(1)

    @pl.when(i == 0)
    def _():
        if has_h0:
            s_scr[...] = h0_ref[0]
        else:
            s_scr[...] = jnp.zeros_like(s_scr)

    d = 1 if reverse else 0
    dt_c0 = d * SSD_HEADS
    la_c0 = 2 * SSD_HEADS + d * SSD_HEADS
    dtla = dtla_ref[0]
    row = lax.broadcasted_iota(jnp.int32, (T, T), 0)
    col = lax.broadcasted_iota(jnp.int32, (T, T), 1)
    keep = (col >= row) if reverse else (col <= row)
    tri = keep.astype(F32)
    cs = jnp.dot(tri, dtla, preferred_element_type=F32, precision=lax.Precision.HIGHEST)
    cs_t = cs.T
    dtla_t = dtla.T
    end_row = 0 if reverse else T - 1
    a_t = cs_t[la_c0:la_c0 + SSD_HEADS]
    a_end_t = cs_t[la_c0:la_c0 + SSD_HEADS, end_row:end_row + 1]
    w_t = dtla_t[dt_c0:dt_c0 + SSD_HEADS] * jnp.exp(a_end_t - a_t)
    dt_t = dtla_t[dt_c0:dt_c0 + SSD_HEADS]

    lane_q = lax.broadcasted_iota(jnp.int32, (1, QW), 1) // SSD_HEADDIM
    lane_pair = lax.broadcasted_iota(jnp.int32, (1, LANES), 1) < SSD_HEADDIM

    for g in range(SSD_GROUPS):
        b_g = xbc_ref[0, :, D_SSD + g * SSD_STATE:D_SSD + (g + 1) * SSD_STATE]
        bt_g = b_g.astype(F32).T
        if with_y:
            c_g = xbc_ref[0, :, D_SSD + D_BC + g * SSD_STATE:D_SSD + D_BC + (g + 1) * SSD_STATE]
            cb_g = _dot_nt(c_g, b_g)
            gc0 = g * SSD_HPG * SSD_HEADDIM
            s_g = s_scr[:, gc0:gc0 + SSD_HPG * SSD_HEADDIM].astype(BF16)
            yoff_g = _dot(c_g, s_g)
        for qd in range(SSD_HPG // QUAD):
            h0 = g * SSD_HPG + qd * QUAD
            c0 = h0 * SSD_HEADDIM
            xq = xbc_ref[0, :, c0:c0 + QW]
            rhs = jnp.concatenate(
                [xq * (lane_q == j).astype(BF16) for j in range(QUAD)], axis=0)
            top, bot, a_rep = [], [], []
            for j in range(QUAD):
                hh = h0 + j
                bot.append((bt_g * w_t[hh:hh + 1]).astype(BF16))
                ar = jnp.broadcast_to(cs[:, la_c0 + hh:la_c0 + hh + 1], (T, LANES))
                a_rep.append(ar)
                if with_y:
                    seg = jnp.where(keep, ar - a_t[hh:hh + 1], NEG_BIG)
                    top.append((cb_g * jnp.exp(seg) * dt_t[hh:hh + 1]).astype(BF16))
            bot_m = jnp.concatenate(bot, axis=1)
            if with_y:
                lhs = jnp.concatenate([jnp.concatenate(top, axis=1), bot_m], axis=0)
            else:
                lhs = bot_m
            res = _dot(lhs, rhs)
            ds = res[T:] if with_y else res
            for p in range(QUAD // 2):
                ar_pair = jnp.where(lane_pair, a_rep[2 * p], a_rep[2 * p + 1])
                pc0 = c0 + p * LANES
                if with_y:
                    yo = yoff_g[:, (qd * QW + p * LANES):(qd * QW + (p + 1) * LANES)]
                    y_scr[:, pc0:pc0 + LANES] = res[:T, p * LANES:(p + 1) * LANES] + yo * jnp.exp(ar_pair)
                decay = jnp.exp(ar_pair[end_row:end_row + 1])
                s_scr[:, pc0:pc0 + LANES] = (s_scr[:, pc0:pc0 + LANES] * decay
                                             + ds[:, p * LANES:(p + 1) * LANES])

    if with_y:
        if fuse_out:
            xs = xbc_ref[0, :, 0:D_SSD].astype(F32)
            y = y_scr[...] + yf_ref[0] + dskip_ref[...] * xs
            gy = y * z_ref[0].astype(F32)
            ms = jnp.mean(gy * gy, axis=-1, keepdims=True)
            y_o[0] = (gy * lax.rsqrt(ms + EPS) * ng_ref[...]).astype(y_o.dtype)
        else:
            y_o[0] = y_scr[...]

    @pl.when(i == n_i - 1)
    def _():
        hfin_o[0] = s_scr[...]


def _ssd_call(xbc, dtla, h0, reverse, with_y, fuse=None):
    bsz, n_rows, _ = xbc.shape
    T = SSD_CHUNK
    nc = n_rows // T
    cmap = (lambda b, i: (b, nc - 1 - i, 0)) if reverse else (lambda b, i: (b, i, 0))
    bmap = lambda b, i: (b, 0, 0)
    in_specs = [pl.BlockSpec((1, T, D_XBC), cmap), pl.BlockSpec((1, T, LANES), cmap)]
    args = [xbc, dtla]
    if h0 is not None:
        in_specs.append(pl.BlockSpec((1, SSD_STATE, D_SSD), bmap))
        args.append(h0)
    if fuse is not None:
        yf, zg, dskip, ng = fuse
        in_specs += [pl.BlockSpec((1, T, D_SSD), cmap), pl.BlockSpec((1, T, D_SSD), cmap),
                     _resident((1, D_SSD)), _resident((1, D_SSD))]
        args += [yf, zg, dskip, ng]
    out_shape, out_specs = [], []
    if with_y:
        out_shape.append(jax.ShapeDtypeStruct((bsz, n_rows, D_SSD), BF16 if fuse is not None else F32))
        out_specs.append(pl.BlockSpec((1, T, D_SSD), cmap))
    out_shape.append(jax.ShapeDtypeStruct((bsz, SSD_STATE, D_SSD), F32))
    out_specs.append(pl.BlockSpec((1, SSD_STATE, D_SSD), bmap))
    scratch = [pltpu.VMEM((SSD_STATE, D_SSD), F32)]
    if with_y:
        scratch.append(pltpu.VMEM((T, D_SSD), F32))
    name = "ssd_" + ("rev" if reverse else "fwd") + ("_y" if with_y else "_state") + ("_out" if fuse else "")
    return pl.pallas_call(
        functools.partial(_ssd_kernel, reverse=reverse, has_h0=h0 is not None, with_y=with_y,
                          fuse_out=fuse is not None),
        grid=(bsz, nc),
        in_specs=in_specs, out_specs=out_specs, out_shape=out_shape,
        scratch_shapes=scratch,
        compiler_params=pltpu.CompilerParams(dimension_semantics=("arbitrary", "arbitrary"),
                                             vmem_limit_bytes=_vmem_limit(40 << 20)),
        name=name,
    )(*args)


def _attn_kernel(q_ref, kp_ref, kc_ref, kn_ref, vp_ref, vc_ref, vn_ref, kx_ref, vx_ref, sink_ref, o_ref):
    blk = ATT_BLOCK
    n = pl.program_id(1)
    n_n = pl.num_programs(1)
    row = lax.broadcasted_iota(jnp.int32, (blk, blk), 0)
    col = lax.broadcasted_iota(jnp.int32, (blk, blk), 1)
    prev_ok = jnp.logical_and(col >= row, n > 0)
    next_ok = jnp.logical_and(col <= row, n < n_n - 1)
    prev_ok = jnp.concatenate([prev_ok] * ATT_KV_HEADS, axis=0)
    next_ok = jnp.concatenate([next_ok] * ATT_KV_HEADS, axis=0)
    lane_kv = lax.broadcasted_iota(jnp.int32, (1, D_KV), 1) // ATT_HEADDIM
    kv_masks = [(lane_kv == j) for j in range(ATT_KV_HEADS)]
    kp, kc, kn, kx = kp_ref[0], kc_ref[0], kn_ref[0], kx_ref[0]
    vp, vc, vn, vx = vp_ref[0], vc_ref[0], vn_ref[0], vx_ref[0]
    for g in range(ATT_GROUP):
        qg = q_ref[0, :, g * D_KV:(g + 1) * D_KV]
        qs = jnp.concatenate([jnp.where(kv_masks[j], qg, jnp.zeros_like(qg))
                              for j in range(ATT_KV_HEADS)], axis=0)
        s_p = jnp.where(prev_ok, _dot_nt(qs, kp), NEG_BIG)
        s_c = _dot_nt(qs, kc)
        s_n = jnp.where(next_ok, _dot_nt(qs, kn), NEG_BIG)
        s_x = _dot_nt(qs, kx)
        sink = jnp.concatenate(
            [jnp.full((blk, 1), sink_ref[j * ATT_GROUP + g], F32) for j in range(ATT_KV_HEADS)], axis=0)
        m = jnp.maximum(jnp.maximum(jnp.max(s_p, axis=-1, keepdims=True), jnp.max(s_c, axis=-1, keepdims=True)),
                        jnp.maximum(jnp.max(s_n, axis=-1, keepdims=True), jnp.max(s_x, axis=-1, keepdims=True)))
        m = jnp.maximum(m, sink)
        p_p, p_c, p_n, p_x = jnp.exp(s_p - m), jnp.exp(s_c - m), jnp.exp(s_n - m), jnp.exp(s_x - m)
        den = (jnp.sum(p_p, axis=-1, keepdims=True) + jnp.sum(p_c, axis=-1, keepdims=True)
               + jnp.sum(p_n, axis=-1, keepdims=True) + jnp.sum(p_x, axis=-1, keepdims=True)
               + jnp.exp(sink - m))
        o = (_dot(p_p.astype(BF16), vp) + _dot(p_c.astype(BF16), vc)
             + _dot(p_n.astype(BF16), vn) + _dot(p_x.astype(BF16), vx))
        o = o / den
        og = jnp.zeros((blk, D_KV), F32)
        for j in range(ATT_KV_HEADS):
            og = jnp.where(kv_masks[j], o[j * blk:(j + 1) * blk], og)
        o_ref[0, :, g * D_KV:(g + 1) * D_KV] = og.astype(o_ref.dtype)


def _attn_call(q, k, v, kx, vx, sink):
    bsz, n_rows, _ = q.shape
    blk = ATT_BLOCK
    nb = n_rows // blk
    lc = kx.shape[1]
    cur = lambda b, n: (b, n, 0)
    prev = lambda b, n: (b, jnp.maximum(n - 1, 0), 0)
    nxt = lambda b, n: (b, jnp.minimum(n + 1, nb - 1), 0)
    ctx = lambda b, n: (b, 0, 0)
    kv_spec = lambda m: pl.BlockSpec((1, blk, D_KV), m)
    return pl.pallas_call(
        _attn_kernel,
        grid=(bsz, nb),
        in_specs=[pl.BlockSpec((1, blk, D_Q), cur),
                  kv_spec(prev), kv_spec(cur), kv_spec(nxt),
                  kv_spec(prev), kv_spec(cur), kv_spec(nxt),
                  pl.BlockSpec((1, lc, D_KV), ctx), pl.BlockSpec((1, lc, D_KV), ctx),
                  pl.BlockSpec(memory_space=pltpu.SMEM)],
        out_specs=pl.BlockSpec((1, blk, D_Q), cur),
        out_shape=jax.ShapeDtypeStruct((bsz, n_rows, D_Q), BF16),
        compiler_params=pltpu.CompilerParams(dimension_semantics=("arbitrary", "arbitrary"),
                                             vmem_limit_bytes=_vmem_limit(40 << 20)),
        name="attn",
    )(q, k, k, k, v, v, v, kx, vx, sink)


def _merge_kernel(y_ref, a_ref, gate_ref, x_ref, mod_ref, wos_ref, woa_ref, wout_ref, o_ref):
    gates = gate_ref[0].astype(F32)
    m = (gates[:, :D_MODEL] * _dot(y_ref[0], wos_ref[...])
         + gates[:, D_MODEL:] * _dot(a_ref[0], woa_ref[...]))
    o = _dot(m.astype(BF16), wout_ref[...])
    gt1 = mod_ref[0, :, 2 * D_MODEL:3 * D_MODEL]
    o_ref[0] = x_ref[0] + gt1 * o


def _merge_call(yn, att, gates, x, modv, w_os, w_oa, w_out, tm):
    bsz, n_rows, _ = x.shape
    row_map = lambda b, i: (b, i, 0)
    return pl.pallas_call(
        _merge_kernel,
        grid=(bsz, n_rows // tm),
        in_specs=[pl.BlockSpec((1, tm, D_SSD), row_map), pl.BlockSpec((1, tm, D_Q), row_map),
                  pl.BlockSpec((1, tm, 2 * D_MODEL), row_map), pl.BlockSpec((1, tm, D_MODEL), row_map),
                  pl.BlockSpec((1, 1, N_MOD * D_MODEL), lambda b, i: (b, 0, 0)),
                  _resident((D_SSD, D_MODEL)), _resident((D_Q, D_MODEL)), _resident((D_MODEL, D_MODEL))],
        out_specs=pl.BlockSpec((1, tm, D_MODEL), row_map),
        out_shape=jax.ShapeDtypeStruct((bsz, n_rows, D_MODEL), F32),
        compiler_params=pltpu.CompilerParams(dimension_semantics=("arbitrary", "arbitrary"),
                                             vmem_limit_bytes=_vmem_limit(48 << 20)),
        name="merge",
    )(yn, att, gates, x, modv, w_os, w_oa, w_out)


FF_CHUNK = 256


def _ffn_kernel(x_ref, xp_ref, xn_ref, mod_ref, g_ref, wup_ref, cw_ref, cb_ref, wdn_ref, fg_ref,
                o_ref, ua_scr, ub_scr, act_scr, *, tm):
    i = pl.program_id(1)
    n_i = pl.num_programs(1)
    g = g_ref[...]
    shift = mod_ref[0, :, 3 * D_MODEL:4 * D_MODEL]
    scale = mod_ref[0, :, 4 * D_MODEL:5 * D_MODEL]
    gt2 = mod_ref[0, :, 5 * D_MODEL:6 * D_MODEL]
    x = x_ref[0]
    h = _modulated_norm(x, g, shift, scale)
    hp = _modulated_norm(xp_ref[0], g, shift, scale) * (i > 0).astype(F32)
    hn = _modulated_norm(xn_ref[0], g, shift, scale) * (i < n_i - 1).astype(F32)
    h_ext = jnp.concatenate([hp, h, hn], axis=0).astype(BF16)
    for c0 in range(0, D_FF, FF_CHUNK):
        ua_scr[...] = _dot(h_ext, wup_ref[:, c0:c0 + FF_CHUNK])
        ub_scr[...] = _dot(h_ext, wup_ref[:, D_FF + c0:D_FF + c0 + FF_CHUNK])
        a = (ua_scr[pl.ds(HALO - 1, tm), :] * cw_ref[0:1, c0:c0 + FF_CHUNK]
             + ua_scr[pl.ds(HALO, tm), :] * cw_ref[1:2, c0:c0 + FF_CHUNK]
             + ua_scr[pl.ds(HALO + 1, tm), :] * cw_ref[2:3, c0:c0 + FF_CHUNK]
             + cb_ref[:, c0:c0 + FF_CHUNK])
        b0 = D_FF + c0
        b = (ub_scr[pl.ds(HALO - 1, tm), :] * cw_ref[0:1, b0:b0 + FF_CHUNK]
             + ub_scr[pl.ds(HALO, tm), :] * cw_ref[1:2, b0:b0 + FF_CHUNK]
             + ub_scr[pl.ds(HALO + 1, tm), :] * cw_ref[2:3, b0:b0 + FF_CHUNK]
             + cb_ref[:, b0:b0 + FF_CHUNK])
        act_scr[:, c0:c0 + FF_CHUNK] = (_silu(a) * b).astype(BF16)
    f = _dot(act_scr[...], wdn_ref[...])
    x2 = x + gt2 * f
    ms = jnp.mean(x2 * x2, axis=-1, keepdims=True)
    o_ref[0] = x2 * lax.rsqrt(ms + EPS) * fg_ref[...]


def _ffn_call(x1, modv, norm_g, w_up, conv_w, conv_b, w_down, final_g, tm):
    bsz, n_rows, _ = x1.shape
    prev_map, next_map = _halo_maps(tm, n_rows)
    row_map = lambda b, i: (b, i, 0)
    return pl.pallas_call(
        functools.partial(_ffn_kernel, tm=tm),
        grid=(bsz, n_rows // tm),
        in_specs=[pl.BlockSpec((1, tm, D_MODEL), row_map),
                  pl.BlockSpec((1, HALO, D_MODEL), prev_map),
                  pl.BlockSpec((1, HALO, D_MODEL), next_map),
                  pl.BlockSpec((1, 1, N_MOD * D_MODEL), lambda b, i: (b, 0, 0)),
                  _resident((1, D_MODEL)),
                  _resident((D_MODEL, 2 * D_FF)), _resident((3, 2 * D_FF)), _resident((1, 2 * D_FF)),
                  _resident((D_FF, D_MODEL)), _resident((1, D_MODEL))],
        out_specs=pl.BlockSpec((1, tm, D_MODEL), row_map),
        out_shape=jax.ShapeDtypeStruct((bsz, n_rows, D_MODEL), F32),
        scratch_shapes=[pltpu.VMEM((tm + 2 * HALO, FF_CHUNK), F32),
                        pltpu.VMEM((tm + 2 * HALO, FF_CHUNK), F32),
                        pltpu.VMEM((tm, D_FF), BF16)],
        compiler_params=pltpu.CompilerParams(dimension_semantics=("arbitrary", "arbitrary"),
                                             vmem_limit_bytes=_vmem_limit(48 << 20)),
        name="ffn",
    )(x1, x1, x1, modv, norm_g, w_up, conv_w, conv_b, w_down, final_g)


def _rope_tables(n_rows):
    half = ATT_HEADDIM // 2
    inv = ROPE_BASE ** (-jnp.arange(0, half, 2, dtype=F32) / half)
    pos = jnp.arange(n_rows)
    rowp = (pos // GRID_W).astype(F32)[:, None] * inv[None]
    colp = (pos % GRID_W).astype(F32)[:, None] * inv[None]
    cos = jnp.concatenate([jnp.cos(rowp), jnp.cos(rowp), jnp.cos(colp), jnp.cos(colp)], axis=-1)
    sin = jnp.concatenate([-jnp.sin(rowp), jnp.sin(rowp), -jnp.sin(colp), jnp.sin(colp)], axis=-1)
    reps = LANES // ATT_HEADDIM
    return jnp.tile(cos, (1, reps)), jnp.tile(sin, (1, reps))


def _q_perm():
    idx = np.arange(D_Q).reshape(ATT_KV_HEADS, ATT_GROUP, ATT_HEADDIM).transpose(1, 0, 2)
    return idx.reshape(-1)


def kernel(x, c, ctx, c_ctx, w_mod, b_mod, norm1_g, norm2_g, w_in, ssd_conv_w, ssd_conv_b, ssd_dt_bias,
           ssd_a_log, ssd_d, ssd_norm_g, w_o_ssd, w_o_att, att_sink, w_out, w_up, ffn_conv_w, ffn_conv_b,
           w_down, final_g):
    depth = w_mod.shape[0]
    assert depth == 1, "single-layer block"
    l = 0
    bsz, n_lat, _ = x.shape
    assert bsz + 1 <= SUBLANES
    qperm = _q_perm()

    cc = jnp.concatenate([c, c_ctx[None], jnp.zeros((SUBLANES - bsz - 1, D_MODEL), F32)], axis=0)
    mod_all = _mod_call(cc, w_mod[l], b_mod[l][None])
    mod_lat = mod_all[:bsz, None, :]
    mod_ctx = mod_all[bsz:bsz + 1, None, :]

    wi = w_in[l].astype(BF16)
    w_dt = wi[:, DT0:Q0]
    w = {"z": wi[:, Z0:XBC0], "xbc": wi[:, XBC0:DT0], "dt": jnp.concatenate([w_dt, w_dt], axis=1),
         "q": wi[:, Q0:K0][:, qperm], "k": wi[:, K0:V0], "v": wi[:, V0:G0], "g": wi[:, G0:D_IN]}
    dtb = ssd_dt_bias[l].reshape(1, 2 * SSD_HEADS)
    a_mult = -jnp.exp(ssd_a_log[l].reshape(1, 2 * SSD_HEADS))
    consts = {"conv_w": ssd_conv_w[l], "conv_b": ssd_conv_b[l][None],
              "dt_bias": jnp.concatenate([dtb, dtb], axis=1),
              "a_mult": jnp.concatenate([jnp.ones_like(a_mult), a_mult], axis=1)}
    rope = _rope_tables(n_lat)
    g1 = norm1_g[l][None]

    xbc_c, dtla_c, k_c, v_c = _inproj_call(ctx, mod_ctx, False, g1, w, consts, None, ctx.shape[1], False)
    xbc_l, dtla_l, k_l, v_l, zg, q_l, gates = _inproj_call(x, mod_lat, True, g1, w, consts, rope, 256, True)

    (hf_c,) = _ssd_call(xbc_c, dtla_c, None, False, False)
    (hb_c,) = _ssd_call(xbc_c, dtla_c, None, True, False)
    yf, _ = _ssd_call(xbc_l, dtla_l, hf_c, False, True)
    dskip = jnp.repeat(ssd_d[l], SSD_HEADDIM)[None]
    yn, _ = _ssd_call(xbc_l, dtla_l, hb_c, True, True, fuse=(yf, zg, dskip, ssd_norm_g[l][None]))

    sink = att_sink[l].astype(F32)
    att = _attn_call(q_l, k_l, v_l, k_c, v_c, sink)

    x1 = _merge_call(yn, att, gates, x, mod_lat, w_o_ssd[l].astype(BF16),
                     w_o_att[l][qperm, :].astype(BF16), w_out[l].astype(BF16), 256)
    return _ffn_call(x1, mod_lat, norm2_g[l][None], w_up[l].astype(BF16), ffn_conv_w[l],
                     ffn_conv_b[l][None], w_down[l].astype(BF16), final_g[None], 256)
```

```python
import functools
import math

import jax
import jax.numpy as jnp
import numpy as np
from jax import lax
from jax.experimental import pallas as pl
from jax.experimental.pallas import tpu as pltpu

F32 = jnp.float32
BF16 = jnp.bfloat16

D_MODEL = 1024
GRID_W = 64
EPS = 1e-6
N_MOD = 6
D_SSD = 2048
SSD_HEADDIM = 64
SSD_HEADS = 32
SSD_GROUPS = 4
SSD_HPG = 8
SSD_STATE = 128
SSD_CHUNK = 128
D_BC = SSD_GROUPS * SSD_STATE
D_XBC = D_SSD + 2 * D_BC
ATT_HEADS = 16
ATT_KV_HEADS = 4
ATT_GROUP = 4
ATT_HEADDIM = 64
ATT_BLOCK = 128
D_Q = ATT_HEADS * ATT_HEADDIM
D_KV = ATT_KV_HEADS * ATT_HEADDIM
ROPE_BASE = 10000.0
D_FF = 2816
Z0 = 0
XBC0 = Z0 + D_SSD
DT0 = XBC0 + D_XBC
Q0 = DT0 + 2 * SSD_HEADS
K0 = Q0 + D_Q
V0 = K0 + D_KV
G0 = V0 + D_KV
D_IN = G0 + 2 * D_MODEL

V7X_VMEM_BYTES = 64 * 1024 * 1024
LANES = 128
SUBLANES = 8
HALO = SUBLANES
NEG_BIG = -1e30
LOG2E = 1.4426950408889634


def _vmem_limit(nbytes):
    return int(min(nbytes, V7X_VMEM_BYTES - 8 * 1024 * 1024))


def _resident(shape):
    nd = len(shape)
    return pl.BlockSpec(shape, lambda *_: (0,) * nd, pipeline_mode=pl.Buffered(1))


def _sigmoid(x):
    return 1.0 / (1.0 + jnp.exp2(x * (-LOG2E)))


def _silu(x):
    return x * _sigmoid(x)


def _dot(a, b):
    return jnp.dot(a, b, preferred_element_type=F32)


def _dot_nt(a, b):
    return lax.dot_general(a, b, (((1,), (1,)), ((), ())), preferred_element_type=F32)


def _modulated_norm(x, g, shift, scale):
    ms = jnp.mean(x * x, axis=-1, keepdims=True)
    return x * lax.rsqrt(ms + EPS) * g * (1.0 + scale) + shift


def _mod_kernel(c_ref, w_ref, b_ref, o_ref):
    s = _silu(c_ref[...])
    o_ref[...] = _dot(s.astype(BF16), w_ref[...].astype(BF16)) + b_ref[...]


def _mod_call(cc, w_mod, b_mod):
    n = w_mod.shape[1]
    tn = 1536
    return pl.pallas_call(
        _mod_kernel,
        grid=(n // tn,),
        in_specs=[pl.BlockSpec((SUBLANES, D_MODEL), lambda j: (0, 0)),
                  pl.BlockSpec((D_MODEL, tn), lambda j: (0, j)),
                  pl.BlockSpec((1, tn), lambda j: (0, j))],
        out_specs=pl.BlockSpec((SUBLANES, tn), lambda j: (0, j)),
        out_shape=jax.ShapeDtypeStruct((SUBLANES, n), F32),
        compiler_params=pltpu.CompilerParams(dimension_semantics=("arbitrary",),
                                             vmem_limit_bytes=_vmem_limit(40 << 20)),
        name="mod",
    )(cc, w_mod, b_mod)


def _halo_maps(tm, n_rows):
    hb = tm // HALO
    last = n_rows // HALO - 1
    prev_map = lambda b, i: (b, jnp.maximum(i * hb - 1, 0), 0)
    next_map = lambda b, i: (b, jnp.minimum((i + 1) * hb, last), 0)
    return prev_map, next_map


def _conv3_silu_from_scratch(u_scr, tm, cw_ref, cb_ref, lo, hi):
    w = cw_ref[:, lo:hi]
    y = (u_scr[pl.ds(HALO - 1, tm), lo:hi] * w[0:1]
         + u_scr[pl.ds(HALO, tm), lo:hi] * w[1:2]
         + u_scr[pl.ds(HALO + 1, tm), lo:hi] * w[2:3])
    return y + cb_ref[:, lo:hi]


def _rope(t, cos, sin_signed, n_tiles):
    width = t.shape[1]
    lane = lax.broadcasted_iota(jnp.int32, t.shape, 1)
    first = (lane % 32) < 16
    partner = jnp.where(first, pltpu.roll(t, width - 16, 1), pltpu.roll(t, 16, 1))
    cos_t = jnp.concatenate([cos] * n_tiles, axis=1)
    sin_t = jnp.concatenate([sin_signed] * n_tiles, axis=1)
    return t * cos_t + partner * sin_t


W_XBC0 = 0
W_DT0 = W_XBC0 + D_XBC
W_K0 = W_DT0 + LANES
W_V0 = W_K0 + D_KV
W_CTX_END = W_V0 + D_KV
W_Z0 = W_CTX_END
W_Q0 = W_Z0 + D_SSD
W_G0 = W_Q0 + D_Q
W_END = W_G0 + 2 * D_MODEL


def _inproj_kernel(*refs, tm, latent):
    if latent:
        (x_ref, xp_ref, xn_ref, mod_ref, g_ref, w_ref, cw_ref, cb_ref, dtb_ref, am_ref, cos_ref, sin_ref,
         xbc_o, dtla_o, k_o, v_o, z_o, q_o, gate_o, u_scr) = refs
    else:
        (x_ref, xp_ref, xn_ref, mod_ref, g_ref, w_ref, cw_ref, cb_ref, dtb_ref, am_ref,
         xbc_o, dtla_o, k_o, v_o, u_scr) = refs
    i = pl.program_id(1)
    n_i = pl.num_programs(1)
    g = g_ref[...]
    shift = mod_ref[0, :, 0:D_MODEL]
    scale = mod_ref[0, :, D_MODEL:2 * D_MODEL]
    h = _modulated_norm(x_ref[0], g, shift, scale)
    hp = _modulated_norm(xp_ref[0], g, shift, scale) * (i > 0).astype(F32)
    hn = _modulated_norm(xn_ref[0], g, shift, scale) * (i < n_i - 1).astype(F32)
    hb = h.astype(BF16)
    h_ext = jnp.concatenate([hp, h, hn], axis=0).astype(BF16)

    u_scr[...] = _dot(h_ext, w_ref[:, W_XBC0:W_DT0])
    cchunk = 512
    for c0 in range(0, D_XBC, cchunk):
        y = _conv3_silu_from_scratch(u_scr, tm, cw_ref, cb_ref, c0, c0 + cchunk)
        xbc_o[0, :, c0:c0 + cchunk] = _silu(y).astype(BF16)

    raw = _dot(hb, w_ref[:, W_DT0:W_K0]) + dtb_ref[...]
    dt = jnp.maximum(raw, 0.0) + jnp.log1p(jnp.exp(-jnp.abs(raw)))
    lane = lax.broadcasted_iota(jnp.int32, dt.shape, 1)
    dtla_o[0] = jnp.where(lane < 2 * SSD_HEADS, dt, dt * am_ref[...])

    kf = _dot(hb, w_ref[:, W_K0:W_V0])
    if latent:
        cos = cos_ref[...]
        sin = sin_ref[...]
        kf = _rope(kf, cos, sin, D_KV // LANES)
    k_o[0] = kf.astype(BF16)
    v_o[0] = _dot(hb, w_ref[:, W_V0:W_CTX_END]).astype(BF16)
    if latent:
        z_o[0] = _silu(_dot(hb, w_ref[:, W_Z0:W_Q0])).astype(BF16)
        qf = _rope(_dot(hb, w_ref[:, W_Q0:W_G0]), cos, sin, D_Q // LANES)
        q_o[0] = (qf * (ATT_HEADDIM ** -0.5 * LOG2E)).astype(BF16)
        gate_o[0] = _sigmoid(_dot(hb, w_ref[:, W_G0:W_END])).astype(BF16)


def _inproj_call(x, modv, per_batch_mod, norm_g, w_all, consts, rope, tm, latent):
    bsz, n_rows, _ = x.shape
    prev_map, next_map = _halo_maps(tm, n_rows)
    row_map = lambda b, i: (b, i, 0)
    mod_map = (lambda b, i: (b, 0, 0)) if per_batch_mod else (lambda b, i: (0, 0, 0))
    in_specs = [pl.BlockSpec((1, tm, D_MODEL), row_map),
                pl.BlockSpec((1, HALO, D_MODEL), prev_map),
                pl.BlockSpec((1, HALO, D_MODEL), next_map),
                pl.BlockSpec((1, 1, N_MOD * D_MODEL), mod_map),
                _resident((1, D_MODEL)),
                _resident((D_MODEL, W_END if latent else W_CTX_END)),
                _resident((3, D_XBC)), _resident((1, D_XBC)),
                _resident((1, LANES)), _resident((1, LANES))]
    args = [x, x, x, modv, norm_g, w_all, consts["conv_w"], consts["conv_b"],
            consts["dt_bias"], consts["a_mult"]]
    out_shape = [jax.ShapeDtypeStruct((bsz, n_rows, D_XBC), BF16),
                 jax.ShapeDtypeStruct((bsz, n_rows, LANES), F32),
                 jax.ShapeDtypeStruct((bsz, n_rows, D_KV), BF16),
                 jax.ShapeDtypeStruct((bsz, n_rows, D_KV), BF16)]
    out_specs = [pl.BlockSpec((1, tm, D_XBC), row_map), pl.BlockSpec((1, tm, LANES), row_map),
                 pl.BlockSpec((1, tm, D_KV), row_map), pl.BlockSpec((1, tm, D_KV), row_map)]
    if latent:
        in_specs += [pl.BlockSpec((tm, LANES), lambda b, i: (i, 0)),
                     pl.BlockSpec((tm, LANES), lambda b, i: (i, 0))]
        args += [rope[0], rope[1]]
        out_shape += [jax.ShapeDtypeStruct((bsz, n_rows, D_SSD), BF16),
                      jax.ShapeDtypeStruct((bsz, n_rows, D_Q), BF16),
                      jax.ShapeDtypeStruct((bsz, n_rows, 2 * D_MODEL), BF16)]
        out_specs += [pl.BlockSpec((1, tm, D_SSD), row_map), pl.BlockSpec((1, tm, D_Q), row_map),
                      pl.BlockSpec((1, tm, 2 * D_MODEL), row_map)]
    return pl.pallas_call(
        functools.partial(_inproj_kernel, tm=tm, latent=latent),
        grid=(bsz, n_rows // tm),
        in_specs=in_specs, out_specs=out_specs, out_shape=out_shape,
        scratch_shapes=[pltpu.VMEM((tm + 2 * HALO, D_XBC), F32)],
        compiler_params=pltpu.CompilerParams(dimension_semantics=("arbitrary", "arbitrary"),
                                             vmem_limit_bytes=_vmem_limit(56 << 20)),
        name="inproj_lat" if latent else "inproj_ctx",
    )(*args)


QUAD = 4
QW = QUAD * SSD_HEADDIM


def _ssd_kernel(*refs, reverse, has_h0, with_y, fuse_out, cps):
    T = SSD_CHUNK
    it = iter(refs)
    xbc_ref = next(it)
    dtla_ref = next(it)
    h0_ref = next(it) if has_h0 else None
    if fuse_out:
        yf_ref, z_ref, dskip_ref, ng_ref = next(it), next(it), next(it), next(it)
    y_o = next(it) if with_y else None
    hfin_o = next(it)
    s_scr = next(it)
    rhs_scr = next(it)
    lhs_scr = next(it)
    y_scr = next(it) if fuse_out else None

    i = pl.program_id(1)
    n_i = pl.num_programs(1)

    @pl.when(i == 0)
    def _():
        if has_h0:
            s_scr[...] = h0_ref[0]
        else:
            s_scr[...] = jnp.zeros_like(s_scr)
        rhs_scr[...] = jnp.zeros_like(rhs_scr)

    d = 1 if reverse else 0
    dt_c0 = d * SSD_HEADS
    la_c0 = 2 * SSD_HEADS + d * SSD_HEADS
    lane = lax.broadcasted_iota(jnp.int32, (T, LANES), 1)
    row = lax.broadcasted_iota(jnp.int32, (T, T), 0)
    col = lax.broadcasted_iota(jnp.int32, (T, T), 1)
    keep = (col >= row) if reverse else (col <= row)
    tri = keep.astype(F32)
    end_row = 0 if reverse else T - 1
    lane_pair = lax.broadcasted_iota(jnp.int32, (1, LANES), 1) < SSD_HEADDIM
    n_quads = SSD_HEADS // QUAD

    def scan_chunk(ck):
        rows = pl.ds(ck * T, T)
        dtla = dtla_ref[0, rows, :]
        dtla2 = jnp.where(lane >= 2 * SSD_HEADS, dtla * LOG2E, dtla)
        cs = jnp.dot(tri, dtla2, preferred_element_type=F32, precision=lax.Precision.HIGHEST)
        cs_t = cs.T
        dtla_t = dtla.T
        a_t = cs_t[la_c0:la_c0 + SSD_HEADS]
        a_end_t = cs_t[la_c0:la_c0 + SSD_HEADS, end_row:end_row + 1]
        dt_t = dtla_t[dt_c0:dt_c0 + SSD_HEADS]
        w_t = dt_t * jnp.exp2(a_end_t - a_t)
        adt_t = a_t - jnp.log2(dt_t)

        def y_put(pc0, val):
            if fuse_out:
                y_scr[ck, :, pc0:pc0 + LANES] = val
            else:
                y_o[0, rows, pc0:pc0 + LANES] = val

        def y_get(pc0):
            if fuse_out:
                return y_scr[ck, :, pc0:pc0 + LANES]
            return y_o[0, rows, pc0:pc0 + LANES]

        for g in range(SSD_GROUPS):
            b_g = xbc_ref[0, rows, D_SSD + g * SSD_STATE:D_SSD + (g + 1) * SSD_STATE]
            bt_g = b_g.astype(F32).T
            if with_y:
                c_g = xbc_ref[0, rows, D_SSD + D_BC + g * SSD_STATE:D_SSD + D_BC + (g + 1) * SSD_STATE]
                cb_g = _dot_nt(c_g, b_g)
                gc0 = g * SSD_HPG * SSD_HEADDIM
                s_g = s_scr[:, gc0:gc0 + SSD_HPG * SSD_HEADDIM].astype(BF16)
                yoff_g = _dot(c_g, s_g)
            for qd in range(SSD_HPG // QUAD):
                h0 = g * SSD_HPG + qd * QUAD
                c0 = h0 * SSD_HEADDIM
                slot = ck * n_quads + g * (SSD_HPG // QUAD) + qd
                for j in range(QUAD):
                    rhs_scr[slot, pl.ds(j * T, T), j * SSD_HEADDIM:(j + 1) * SSD_HEADDIM] = (
                        xbc_ref[0, rows, c0 + j * SSD_HEADDIM:c0 + (j + 1) * SSD_HEADDIM])
                boff = T if with_y else 0
                decays = []
                ar_even = None
                for j in range(QUAD):
                    hh = h0 + j
                    lhs_scr[slot, boff:boff + SSD_STATE, j * T:(j + 1) * T] = (
                        bt_g * w_t[hh:hh + 1]).astype(BF16)
                    ar = jnp.broadcast_to(cs[:, la_c0 + hh:la_c0 + hh + 1], (T, LANES))
                    if with_y:
                        seg = jnp.where(keep, ar - adt_t[hh:hh + 1], NEG_BIG)
                        lhs_scr[slot, 0:T, j * T:(j + 1) * T] = (cb_g * jnp.exp2(seg)).astype(BF16)
                    if j % 2 == 0:
                        ar_even = ar
                        continue
                    ar_pair = jnp.where(lane_pair, ar_even, ar)
                    pc0 = c0 + (j // 2) * LANES
                    if with_y:
                        yo = yoff_g[:, pc0 - gc0:pc0 - gc0 + LANES]
                        y_put(pc0, yo * jnp.exp2(ar_pair))
                    decays.append(jnp.exp2(ar_pair[end_row:end_row + 1]))
                res = _dot(lhs_scr[slot], rhs_scr[slot])
                for p in range(QUAD // 2):
                    pc0 = c0 + p * LANES
                    if with_y:
                        y_put(pc0, y_get(pc0) + res[:T, p * LANES:(p + 1) * LANES])
                    s_scr[:, pc0:pc0 + LANES] = (s_scr[:, pc0:pc0 + LANES] * decays[p]
                                                 + res[boff:, p * LANES:(p + 1) * LANES])

        if fuse_out:
            xs = xbc_ref[0, rows, 0:D_SSD].astype(F32)
            y = y_scr[ck] + yf_ref[0, rows, :] + dskip_ref[...] * xs
            gy = y * z_ref[0, rows, :].astype(F32)
            ms = jnp.mean(gy * gy, axis=-1, keepdims=True)
            y_o[0, rows, :] = (gy * lax.rsqrt(ms + EPS) * ng_ref[...]).astype(y_o.dtype)

    for ck in (range(cps - 1, -1, -1) if reverse else range(cps)):
        scan_chunk(ck)

    @pl.when(i == n_i - 1)
    def _():
        hfin_o[0] = s_scr[...]


def _ssd_call(xbc, dtla, h0, reverse, with_y, fuse=None, cps=2):
    bsz, n_rows, _ = xbc.shape
    T = SSD_CHUNK
    rows = cps * T
    ns = n_rows // rows
    cmap = (lambda b, i: (b, ns - 1 - i, 0)) if reverse else (lambda b, i: (b, i, 0))
    bmap = lambda b, i: (b, 0, 0)
    in_specs = [pl.BlockSpec((1, rows, D_XBC), cmap), pl.BlockSpec((1, rows, LANES), cmap)]
    args = [xbc, dtla]
    if h0 is not None:
        in_specs.append(pl.BlockSpec((1, SSD_STATE, D_SSD), bmap))
        args.append(h0)
    if fuse is not None:
        yf, zg, dskip, ng = fuse
        in_specs += [pl.BlockSpec((1, rows, D_SSD), cmap), pl.BlockSpec((1, rows, D_SSD), cmap),
                     _resident((1, D_SSD)), _resident((1, D_SSD))]
        args += [yf, zg, dskip, ng]
    out_shape, out_specs = [], []
    if with_y:
        out_shape.append(jax.ShapeDtypeStruct((bsz, n_rows, D_SSD), BF16 if fuse is not None else F32))
        out_specs.append(pl.BlockSpec((1, rows, D_SSD), cmap))
    out_shape.append(jax.ShapeDtypeStruct((bsz, SSD_STATE, D_SSD), F32))
    out_specs.append(pl.BlockSpec((1, SSD_STATE, D_SSD), bmap))
    scratch = [pltpu.VMEM((SSD_STATE, D_SSD), F32),
               pltpu.VMEM((cps * SSD_HEADS // QUAD, QUAD * T, QW), BF16),
               pltpu.VMEM((cps * SSD_HEADS // QUAD, (T if with_y else 0) + SSD_STATE, QUAD * T), BF16)]
    if fuse is not None:
        scratch.append(pltpu.VMEM((cps, T, D_SSD), F32))
    name = "ssd_" + ("rev" if reverse else "fwd") + ("_y" if with_y else "_state") + ("_out" if fuse else "")
    return pl.pallas_call(
        functools.partial(_ssd_kernel, reverse=reverse, has_h0=h0 is not None, with_y=with_y,
                          fuse_out=fuse is not None, cps=cps),
        grid=(bsz, ns),
        in_specs=in_specs, out_specs=out_specs, out_shape=out_shape,
        scratch_shapes=scratch,
        compiler_params=pltpu.CompilerParams(dimension_semantics=("arbitrary", "arbitrary"),
                                             vmem_limit_bytes=_vmem_limit(40 << 20)),
        name=name,
    )(*args)


ATT_STRIP = 16


def _attn_kernel(q_ref, kp_ref, kc_ref, kn_ref, vp_ref, vc_ref, vn_ref, kx_ref, vx_ref, sink_ref, o_ref,
                 s_scr, p_scr, bias_scr):
    blk = ATT_BLOCK
    n = pl.program_id(1)
    n_n = pl.num_programs(1)
    row = lax.broadcasted_iota(jnp.int32, (blk, blk), 0)
    col = lax.broadcasted_iota(jnp.int32, (blk, blk), 1)
    bias_scr[0] = jnp.where(jnp.logical_and(col >= row, n > 0), 0.0, NEG_BIG)
    bias_scr[1] = jnp.where(jnp.logical_and(col <= row, n < n_n - 1), 0.0, NEG_BIG)
    lane_kv = lax.broadcasted_iota(jnp.int32, (1, D_KV), 1) // ATT_HEADDIM
    kv_masks = [(lane_kv == j) for j in range(ATT_KV_HEADS)]
    kcat = jnp.concatenate([kp_ref[0], kc_ref[0], kn_ref[0], kx_ref[0]], axis=0)
    vcat = jnp.concatenate([vp_ref[0], vc_ref[0], vn_ref[0], vx_ref[0]], axis=0)
    nk = kcat.shape[0]
    for g in range(ATT_GROUP):
        slot = g
        qg = q_ref[0, :, g * D_KV:(g + 1) * D_KV]
        qs = jnp.concatenate([jnp.where(kv_masks[j], qg, jnp.zeros_like(qg))
                              for j in range(ATT_KV_HEADS)], axis=0)
        s_scr[slot] = _dot_nt(qs, kcat)
        for st in range(ATT_KV_HEADS * blk // ATT_STRIP):
            r0 = st * ATT_STRIP
            i0 = r0 % blk
            sink = sink_ref[(r0 // blk) * ATT_GROUP + g] * LOG2E
            rows = pl.ds(r0, ATT_STRIP)
            parts = [s_scr[slot, rows, 0:blk] + bias_scr[0, pl.ds(i0, ATT_STRIP), :],
                     s_scr[slot, rows, blk:2 * blk],
                     s_scr[slot, rows, 2 * blk:3 * blk] + bias_scr[1, pl.ds(i0, ATT_STRIP), :]]
            parts += [s_scr[slot, rows, c0:c0 + LANES] for c0 in range(3 * blk, nk, LANES)]
            mel = parts[0]
            for t in parts[1:]:
                mel = jnp.maximum(mel, t)
            m = jnp.maximum(jnp.max(mel, axis=-1, keepdims=True), sink)
            ps = [jnp.exp2(t - m) for t in parts]
            tot = ps[0]
            for t in ps[1:]:
                tot = tot + t
            den = jnp.sum(tot, axis=-1, keepdims=True) + jnp.exp2(sink - m)
            inv = 1.0 / den
            for ci, t in enumerate(ps):
                p_scr[slot, rows, ci * LANES:(ci + 1) * LANES] = (t * inv).astype(BF16)
        o = _dot(p_scr[slot], vcat)
        og = jnp.zeros((blk, D_KV), F32)
        for j in range(ATT_KV_HEADS):
            og = jnp.where(kv_masks[j], o[j * blk:(j + 1) * blk], og)
        o_ref[0, :, g * D_KV:(g + 1) * D_KV] = og.astype(o_ref.dtype)


def _attn_call(q, k, v, kx, vx, sink):
    bsz, n_rows, _ = q.shape
    blk = ATT_BLOCK
    nb = n_rows // blk
    lc = kx.shape[1]
    cur = lambda b, n: (b, n, 0)
    prev = lambda b, n: (b, jnp.maximum(n - 1, 0), 0)
    nxt = lambda b, n: (b, jnp.minimum(n + 1, nb - 1), 0)
    ctx = lambda b, n: (b, 0, 0)
    kv_spec = lambda m: pl.BlockSpec((1, blk, D_KV), m)
    return pl.pallas_call(
        _attn_kernel,
        grid=(bsz, nb),
        in_specs=[pl.BlockSpec((1, blk, D_Q), cur),
                  kv_spec(prev), kv_spec(cur), kv_spec(nxt),
                  kv_spec(prev), kv_spec(cur), kv_spec(nxt),
                  pl.BlockSpec((1, lc, D_KV), ctx), pl.BlockSpec((1, lc, D_KV), ctx),
                  pl.BlockSpec(memory_space=pltpu.SMEM)],
        out_specs=pl.BlockSpec((1, blk, D_Q), cur),
        out_shape=jax.ShapeDtypeStruct((bsz, n_rows, D_Q), BF16),
        scratch_shapes=[pltpu.VMEM((ATT_GROUP, ATT_KV_HEADS * blk, 3 * blk + lc), F32),
                        pltpu.VMEM((ATT_GROUP, ATT_KV_HEADS * blk, 3 * blk + lc), BF16),
                        pltpu.VMEM((2, blk, blk), F32)],
        compiler_params=pltpu.CompilerParams(dimension_semantics=("arbitrary", "arbitrary"),
                                             vmem_limit_bytes=_vmem_limit(40 << 20)),
        name="attn",
    )(q, k, k, k, v, v, v, kx, vx, sink)


def _merge_kernel(y_ref, a_ref, gate_ref, x_ref, mod_ref, wos_ref, woa_ref, wout_ref, o_ref):
    gates = gate_ref[0].astype(F32)
    m = (gates[:, :D_MODEL] * _dot(y_ref[0], wos_ref[...])
         + gates[:, D_MODEL:] * _dot(a_ref[0], woa_ref[...]))
    o = _dot(m.astype(BF16), wout_ref[...])
    gt1 = mod_ref[0, :, 2 * D_MODEL:3 * D_MODEL]
    o_ref[0] = x_ref[0] + gt1 * o


def _merge_call(yn, att, gates, x, modv, w_os, w_oa, w_out, tm):
    bsz, n_rows, _ = x.shape
    row_map = lambda b, i: (b, i, 0)
    return pl.pallas_call(
        _merge_kernel,
        grid=(bsz, n_rows // tm),
        in_specs=[pl.BlockSpec((1, tm, D_SSD), row_map), pl.BlockSpec((1, tm, D_Q), row_map),
                  pl.BlockSpec((1, tm, 2 * D_MODEL), row_map), pl.BlockSpec((1, tm, D_MODEL), row_map),
                  pl.BlockSpec((1, 1, N_MOD * D_MODEL), lambda b, i: (b, 0, 0)),
                  _resident((D_SSD, D_MODEL)), _resident((D_Q, D_MODEL)), _resident((D_MODEL, D_MODEL))],
        out_specs=pl.BlockSpec((1, tm, D_MODEL), row_map),
        out_shape=jax.ShapeDtypeStruct((bsz, n_rows, D_MODEL), F32),
        compiler_params=pltpu.CompilerParams(dimension_semantics=("arbitrary", "arbitrary"),
                                             vmem_limit_bytes=_vmem_limit(48 << 20)),
        name="merge",
    )(yn, att, gates, x, modv, w_os, w_oa, w_out)


FF_CHUNK = 256


def _ffn_kernel(x_ref, xp_ref, xn_ref, mod_ref, g_ref, wup_ref, cw_ref, cb_ref, wdn_ref, fg_ref,
                o_ref, ua_scr, ub_scr, act_scr, *, tm):
    i = pl.program_id(1)
    n_i = pl.num_programs(1)
    g = g_ref[...]
    shift = mod_ref[0, :, 3 * D_MODEL:4 * D_MODEL]
    scale = mod_ref[0, :, 4 * D_MODEL:5 * D_MODEL]
    gt2 = mod_ref[0, :, 5 * D_MODEL:6 * D_MODEL]
    x = x_ref[0]
    h = _modulated_norm(x, g, shift, scale)
    hp = _modulated_norm(xp_ref[0], g, shift, scale) * (i > 0).astype(F32)
    hn = _modulated_norm(xn_ref[0], g, shift, scale) * (i < n_i - 1).astype(F32)
    h_ext = jnp.concatenate([hp, h, hn], axis=0).astype(BF16)
    for c0 in range(0, D_FF, FF_CHUNK):
        ua_scr[...] = _dot(h_ext, wup_ref[:, c0:c0 + FF_CHUNK])
        ub_scr[...] = _dot(h_ext, wup_ref[:, D_FF + c0:D_FF + c0 + FF_CHUNK])
        a = (ua_scr[pl.ds(HALO - 1, tm), :] * cw_ref[0:1, c0:c0 + FF_CHUNK]
             + ua_scr[pl.ds(HALO, tm), :] * cw_ref[1:2, c0:c0 + FF_CHUNK]
             + ua_scr[pl.ds(HALO + 1, tm), :] * cw_ref[2:3, c0:c0 + FF_CHUNK]
             + cb_ref[:, c0:c0 + FF_CHUNK])
        b0 = D_FF + c0
        b = (ub_scr[pl.ds(HALO - 1, tm), :] * cw_ref[0:1, b0:b0 + FF_CHUNK]
             + ub_scr[pl.ds(HALO, tm), :] * cw_ref[1:2, b0:b0 + FF_CHUNK]
             + ub_scr[pl.ds(HALO + 1, tm), :] * cw_ref[2:3, b0:b0 + FF_CHUNK]
             + cb_ref[:, b0:b0 + FF_CHUNK])
        act_scr[:, c0:c0 + FF_CHUNK] = (_silu(a) * b).astype(BF16)
    f = _dot(act_scr[...], wdn_ref[...])
    x2 = x + gt2 * f
    ms = jnp.mean(x2 * x2, axis=-1, keepdims=True)
    o_ref[0] = x2 * lax.rsqrt(ms + EPS) * fg_ref[...]


def _ffn_call(x1, modv, norm_g, w_up, conv_w, conv_b, w_down, final_g, tm):
    bsz, n_rows, _ = x1.shape
    prev_map, next_map = _halo_maps(tm, n_rows)
    row_map = lambda b, i: (b, i, 0)
    return pl.pallas_call(
        functools.partial(_ffn_kernel, tm=tm),
        grid=(bsz, n_rows // tm),
        in_specs=[pl.BlockSpec((1, tm, D_MODEL), row_map),
                  pl.BlockSpec((1, HALO, D_MODEL), prev_map),
                  pl.BlockSpec((1, HALO, D_MODEL), next_map),
                  pl.BlockSpec((1, 1, N_MOD * D_MODEL), lambda b, i: (b, 0, 0)),
                  _resident((1, D_MODEL)),
                  _resident((D_MODEL, 2 * D_FF)), _resident((3, 2 * D_FF)), _resident((1, 2 * D_FF)),
                  _resident((D_FF, D_MODEL)), _resident((1, D_MODEL))],
        out_specs=pl.BlockSpec((1, tm, D_MODEL), row_map),
        out_shape=jax.ShapeDtypeStruct((bsz, n_rows, D_MODEL), F32),
        scratch_shapes=[pltpu.VMEM((tm + 2 * HALO, FF_CHUNK), F32),
                        pltpu.VMEM((tm + 2 * HALO, FF_CHUNK), F32),
                        pltpu.VMEM((tm, D_FF), BF16)],
        compiler_params=pltpu.CompilerParams(dimension_semantics=("arbitrary", "arbitrary"),
                                             vmem_limit_bytes=_vmem_limit(48 << 20)),
        name="ffn",
    )(x1, x1, x1, modv, norm_g, w_up, conv_w, conv_b, w_down, final_g)


def _rope_tables(n_rows):
    half = ATT_HEADDIM // 2
    inv = ROPE_BASE ** (-jnp.arange(0, half, 2, dtype=F32) / half)
    pos = jnp.arange(n_rows)
    rowp = (pos // GRID_W).astype(F32)[:, None] * inv[None]
    colp = (pos % GRID_W).astype(F32)[:, None] * inv[None]
    cos = jnp.concatenate([jnp.cos(rowp), jnp.cos(rowp), jnp.cos(colp), jnp.cos(colp)], axis=-1)
    sin = jnp.concatenate([-jnp.sin(rowp), jnp.sin(rowp), -jnp.sin(colp), jnp.sin(colp)], axis=-1)
    reps = LANES // ATT_HEADDIM
    return jnp.tile(cos, (1, reps)), jnp.tile(sin, (1, reps))


def kernel(x, c, ctx, c_ctx, w_mod, b_mod, norm1_g, norm2_g, w_in, ssd_conv_w, ssd_conv_b, ssd_dt_bias,
           ssd_a_log, ssd_d, ssd_norm_g, w_o_ssd, w_o_att, att_sink, w_out, w_up, ffn_conv_w, ffn_conv_b,
           w_down, final_g):
    depth = w_mod.shape[0]
    assert depth == 1, "single-layer block"
    l = 0
    bsz, n_lat, _ = x.shape
    assert bsz + 1 <= SUBLANES

    cc = jnp.concatenate([c, c_ctx[None], jnp.zeros((SUBLANES - bsz - 1, D_MODEL), F32)], axis=0)
    mod_all = _mod_call(cc, w_mod[l], b_mod[l][None])
    mod_lat = mod_all[:bsz, None, :]
    mod_ctx = mod_all[bsz:bsz + 1, None, :]

    wi = w_in[l]
    w_q = (wi[:, Q0:K0].reshape(D_MODEL, ATT_KV_HEADS, ATT_GROUP, ATT_HEADDIM)
           .transpose(0, 2, 1, 3).reshape(D_MODEL, D_Q))
    w = jnp.concatenate([wi[:, XBC0:DT0], wi[:, DT0:Q0], wi[:, DT0:Q0], wi[:, K0:G0],
                         wi[:, Z0:XBC0], w_q, wi[:, G0:D_IN]], axis=1).astype(BF16)
    dtb = ssd_dt_bias[l].reshape(1, 2 * SSD_HEADS)
    a_mult = -jnp.exp(ssd_a_log[l].reshape(1, 2 * SSD_HEADS))
    consts = {"conv_w": ssd_conv_w[l], "conv_b": ssd_conv_b[l][None],
              "dt_bias": jnp.concatenate([dtb, dtb], axis=1),
              "a_mult": jnp.concatenate([jnp.ones_like(a_mult), a_mult], axis=1)}
    rope = _rope_tables(n_lat)
    g1 = norm1_g[l][None]

    xbc_c, dtla_c, k_c, v_c = _inproj_call(ctx, mod_ctx, False, g1, w, consts, None, ctx.shape[1], False)
    xbc_l, dtla_l, k_l, v_l, zg, q_l, gates = _inproj_call(x, mod_lat, True, g1, w, consts, rope, 256, True)

    (hf_c,) = _ssd_call(xbc_c, dtla_c, None, False, False)
    (hb_c,) = _ssd_call(xbc_c, dtla_c, None, True, False)
    yf, _ = _ssd_call(xbc_l, dtla_l, hf_c, False, True)
    dskip = jnp.repeat(ssd_d[l], SSD_HEADDIM)[None]
    yn, _ = _ssd_call(xbc_l, dtla_l, hb_c, True, True, fuse=(yf, zg, dskip, ssd_norm_g[l][None]))

    sink = att_sink[l].astype(F32)
    att = _attn_call(q_l, k_l, v_l, k_c, v_c, sink)

    w_oa = (w_o_att[l].reshape(ATT_KV_HEADS, ATT_GROUP, ATT_HEADDIM, D_MODEL)
            .transpose(1, 0, 2, 3).reshape(D_Q, D_MODEL))
    x1 = _merge_call(yn, att, gates, x, mod_lat, w_o_ssd[l].astype(BF16),
                     w_oa.astype(BF16), w_out[l].astype(BF16), 256)
    return _ffn_call(x1, mod_lat, norm2_g[l][None], w_up[l].astype(BF16), ffn_conv_w[l],
                     ffn_conv_b[l][None], w_down[l].astype(BF16), final_g[None], 256)
```

```python
import functools
import math

import jax
import jax.numpy as jnp
import numpy as np
from jax import lax
from jax.experimental import pallas as pl
from jax.experimental.pallas import tpu as pltpu

F32 = jnp.float32
BF16 = jnp.bfloat16

D_MODEL = 1024
GRID_W = 64
EPS = 1e-6
N_MOD = 6
D_SSD = 2048
SSD_HEADDIM = 64
SSD_HEADS = 32
SSD_GROUPS = 4
SSD_HPG = 8
SSD_STATE = 128
SSD_CHUNK = 128
D_BC = SSD_GROUPS * SSD_STATE
D_XBC = D_SSD + 2 * D_BC
ATT_HEADS = 16
ATT_KV_HEADS = 4
ATT_GROUP = 4
ATT_HEADDIM = 64
ATT_BLOCK = 128
D_Q = ATT_HEADS * ATT_HEADDIM
D_KV = ATT_KV_HEADS * ATT_HEADDIM
ROPE_BASE = 10000.0
D_FF = 2816
Z0 = 0
XBC0 = Z0 + D_SSD
DT0 = XBC0 + D_XBC
Q0 = DT0 + 2 * SSD_HEADS
K0 = Q0 + D_Q
V0 = K0 + D_KV
G0 = V0 + D_KV
D_IN = G0 + 2 * D_MODEL

V7X_VMEM_BYTES = 64 * 1024 * 1024
LANES = 128
SUBLANES = 8
HALO = SUBLANES
NEG_BIG = -1e30
LOG2E = 1.4426950408889634


def _vmem_limit(nbytes):
    return int(min(nbytes, V7X_VMEM_BYTES - 8 * 1024 * 1024))


def _resident(shape):
    nd = len(shape)
    return pl.BlockSpec(shape, lambda *_: (0,) * nd, pipeline_mode=pl.Buffered(1))


def _sigmoid(x):
    return 1.0 / (1.0 + jnp.exp2(x * (-LOG2E)))


def _silu(x):
    return x * _sigmoid(x)


def _dot(a, b):
    return jnp.dot(a, b, preferred_element_type=F32)


def _dot_nt(a, b):
    return lax.dot_general(a, b, (((1,), (1,)), ((), ())), preferred_element_type=F32)


def _modulated_norm(x, g, shift, scale):
    ms = jnp.mean(x * x, axis=-1, keepdims=True)
    return x * lax.rsqrt(ms + EPS) * g * (1.0 + scale) + shift


def _mod_kernel(c_ref, w_ref, b_ref, o_ref):
    s = _silu(c_ref[...])
    o_ref[...] = _dot(s.astype(BF16), w_ref[...].astype(BF16)) + b_ref[...]


def _mod_call(cc, w_mod, b_mod):
    n = w_mod.shape[1]
    tn = 1536
    return pl.pallas_call(
        _mod_kernel,
        grid=(n // tn,),
        in_specs=[pl.BlockSpec((SUBLANES, D_MODEL), lambda j: (0, 0)),
                  pl.BlockSpec((D_MODEL, tn), lambda j: (0, j)),
                  pl.BlockSpec((1, tn), lambda j: (0, j))],
        out_specs=pl.BlockSpec((SUBLANES, tn), lambda j: (0, j)),
        out_shape=jax.ShapeDtypeStruct((SUBLANES, n), F32),
        compiler_params=pltpu.CompilerParams(dimension_semantics=("arbitrary",),
                                             vmem_limit_bytes=_vmem_limit(40 << 20)),
        name="mod",
    )(cc, w_mod, b_mod)


def _halo_maps(tm, n_rows):
    hb = tm // HALO
    last = n_rows // HALO - 1
    prev_map = lambda b, i: (b, jnp.maximum(i * hb - 1, 0), 0)
    next_map = lambda b, i: (b, jnp.minimum((i + 1) * hb, last), 0)
    return prev_map, next_map


def _conv3_silu_from_scratch(u_scr, tm, cw_ref, cb_ref, lo, hi):
    w = cw_ref[:, lo:hi]
    y = (u_scr[pl.ds(HALO - 1, tm), lo:hi] * w[0:1]
         + u_scr[pl.ds(HALO, tm), lo:hi] * w[1:2]
         + u_scr[pl.ds(HALO + 1, tm), lo:hi] * w[2:3])
    return y + cb_ref[:, lo:hi]


def _rope(t, cos, sin_signed, n_tiles):
    width = t.shape[1]
    lane = lax.broadcasted_iota(jnp.int32, t.shape, 1)
    first = (lane % 32) < 16
    partner = jnp.where(first, pltpu.roll(t, width - 16, 1), pltpu.roll(t, 16, 1))
    cos_t = jnp.concatenate([cos] * n_tiles, axis=1)
    sin_t = jnp.concatenate([sin_signed] * n_tiles, axis=1)
    return t * cos_t + partner * sin_t


W_XBC0 = 0
W_DT0 = W_XBC0 + D_XBC
W_K0 = W_DT0 + LANES
W_V0 = W_K0 + D_KV
W_CTX_END = W_V0 + D_KV
W_Z0 = W_CTX_END
W_Q0 = W_Z0 + D_SSD
W_G0 = W_Q0 + D_Q
W_END = W_G0 + 2 * D_MODEL


def _inproj_kernel(*refs, tm, latent):
    if latent:
        (x_ref, xp_ref, xn_ref, mod_ref, g_ref, w_ref, cw_ref, cb_ref, dtb_ref, am_ref, cos_ref, sin_ref,
         xbc_o, dtla_o, k_o, v_o, z_o, q_o, gate_o, u_scr, p_scr) = refs
    else:
        (x_ref, xp_ref, xn_ref, mod_ref, g_ref, w_ref, cw_ref, cb_ref, dtb_ref, am_ref,
         xbc_o, dtla_o, k_o, v_o, u_scr, p_scr) = refs
    i = pl.program_id(1)
    n_i = pl.num_programs(1)
    g = g_ref[...]
    shift = mod_ref[0, :, 0:D_MODEL]
    scale = mod_ref[0, :, D_MODEL:2 * D_MODEL]
    h = _modulated_norm(x_ref[0], g, shift, scale)
    hp = _modulated_norm(xp_ref[0], g, shift, scale) * (i > 0).astype(F32)
    hn = _modulated_norm(xn_ref[0], g, shift, scale) * (i < n_i - 1).astype(F32)
    hb = h.astype(BF16)
    h_ext = jnp.concatenate([hp, h, hn], axis=0).astype(BF16)

    u_scr[...] = _dot(h_ext, w_ref[:, W_XBC0:W_DT0])
    p_scr[...] = _dot(hb, w_ref[:, W_DT0:(W_END if latent else W_CTX_END)])
    p0 = W_DT0

    cchunk = 512
    for c0 in range(0, D_XBC, cchunk):
        y = _conv3_silu_from_scratch(u_scr, tm, cw_ref, cb_ref, c0, c0 + cchunk)
        xbc_o[0, :, c0:c0 + cchunk] = _silu(y).astype(BF16)

    raw = p_scr[:, W_DT0 - p0:W_K0 - p0] + dtb_ref[...]
    dt = jnp.maximum(raw, 0.0) + jnp.log1p(jnp.exp(-jnp.abs(raw)))
    lane = lax.broadcasted_iota(jnp.int32, dt.shape, 1)
    dtla_o[0] = jnp.where(lane < 2 * SSD_HEADS, dt, dt * am_ref[...])

    kf = p_scr[:, W_K0 - p0:W_V0 - p0]
    if latent:
        cos = cos_ref[...]
        sin = sin_ref[...]
        kf = _rope(kf, cos, sin, D_KV // LANES)
    k_o[0] = kf.astype(BF16)
    v_o[0] = p_scr[:, W_V0 - p0:W_CTX_END - p0].astype(BF16)
    if latent:
        z_o[0] = _silu(p_scr[:, W_Z0 - p0:W_Q0 - p0]).astype(BF16)
        qf = _rope(p_scr[:, W_Q0 - p0:W_G0 - p0], cos, sin, D_Q // LANES)
        q_o[0] = (qf * (ATT_HEADDIM ** -0.5 * LOG2E)).astype(BF16)
        gate_o[0] = _sigmoid(p_scr[:, W_G0 - p0:W_END - p0]).astype(BF16)


def _inproj_call(x, modv, per_batch_mod, norm_g, w_all, consts, rope, tm, latent):
    bsz, n_rows, _ = x.shape
    prev_map, next_map = _halo_maps(tm, n_rows)
    row_map = lambda b, i: (b, i, 0)
    mod_map = (lambda b, i: (b, 0, 0)) if per_batch_mod else (lambda b, i: (0, 0, 0))
    in_specs = [pl.BlockSpec((1, tm, D_MODEL), row_map),
                pl.BlockSpec((1, HALO, D_MODEL), prev_map),
                pl.BlockSpec((1, HALO, D_MODEL), next_map),
                pl.BlockSpec((1, 1, N_MOD * D_MODEL), mod_map),
                _resident((1, D_MODEL)),
                _resident((D_MODEL, W_END if latent else W_CTX_END)),
                _resident((3, D_XBC)), _resident((1, D_XBC)),
                _resident((1, LANES)), _resident((1, LANES))]
    args = [x, x, x, modv, norm_g, w_all, consts["conv_w"], consts["conv_b"],
            consts["dt_bias"], consts["a_mult"]]
    out_shape = [jax.ShapeDtypeStruct((bsz, n_rows, D_XBC), BF16),
                 jax.ShapeDtypeStruct((bsz, n_rows, LANES), F32),
                 jax.ShapeDtypeStruct((bsz, n_rows, D_KV), BF16),
                 jax.ShapeDtypeStruct((bsz, n_rows, D_KV), BF16)]
    out_specs = [pl.BlockSpec((1, tm, D_XBC), row_map), pl.BlockSpec((1, tm, LANES), row_map),
                 pl.BlockSpec((1, tm, D_KV), row_map), pl.BlockSpec((1, tm, D_KV), row_map)]
    if latent:
        in_specs += [pl.BlockSpec((tm, LANES), lambda b, i: (i, 0)),
                     pl.BlockSpec((tm, LANES), lambda b, i: (i, 0))]
        args += [rope[0], rope[1]]
        out_shape += [jax.ShapeDtypeStruct((bsz, n_rows, D_SSD), BF16),
                      jax.ShapeDtypeStruct((bsz, n_rows, D_Q), BF16),
                      jax.ShapeDtypeStruct((bsz, n_rows, 2 * D_MODEL), BF16)]
        out_specs += [pl.BlockSpec((1, tm, D_SSD), row_map), pl.BlockSpec((1, tm, D_Q), row_map),
                      pl.BlockSpec((1, tm, 2 * D_MODEL), row_map)]
    return pl.pallas_call(
        functools.partial(_inproj_kernel, tm=tm, latent=latent),
        grid=(bsz, n_rows // tm),
        in_specs=in_specs, out_specs=out_specs, out_shape=out_shape,
        scratch_shapes=[pltpu.VMEM((tm + 2 * HALO, D_XBC), F32),
                        pltpu.VMEM((tm, (W_END if latent else W_CTX_END) - W_DT0), F32)],
        compiler_params=pltpu.CompilerParams(dimension_semantics=("arbitrary", "arbitrary"),
                                             vmem_limit_bytes=_vmem_limit(56 << 20)),
        name="inproj_lat" if latent else "inproj_ctx",
    )(*args)


QUAD = 4
QW = QUAD * SSD_HEADDIM


def _ssd_kernel(*refs, reverse, has_h0, with_y, fuse_out, cps):
    T = SSD_CHUNK
    it = iter(refs)
    xbc_ref = next(it)
    dtla_ref = next(it)
    h0_ref = next(it) if has_h0 else None
    if fuse_out:
        yf_ref, z_ref, dskip_ref, ng_ref = next(it), next(it), next(it), next(it)
    y_o = next(it) if with_y else None
    hfin_o = next(it)
    s_scr = next(it)
    rhs_scr = next(it)
    lhs_scr = next(it)
    y_scr = next(it) if fuse_out else None

    i = pl.program_id(1)
    n_i = pl.num_programs(1)

    @pl.when(i == 0)
    def _():
        if has_h0:
            s_scr[...] = h0_ref[0]
        else:
            s_scr[...] = jnp.zeros_like(s_scr)
        rhs_scr[...] = jnp.zeros_like(rhs_scr)

    d = 1 if reverse else 0
    dt_c0 = d * SSD_HEADS
    la_c0 = 2 * SSD_HEADS + d * SSD_HEADS
    lane = lax.broadcasted_iota(jnp.int32, (T, LANES), 1)
    row = lax.broadcasted_iota(jnp.int32, (T, T), 0)
    col = lax.broadcasted_iota(jnp.int32, (T, T), 1)
    keep = (col >= row) if reverse else (col <= row)
    tri = keep.astype(F32)
    end_row = 0 if reverse else T - 1
    lane_pair = lax.broadcasted_iota(jnp.int32, (1, LANES), 1) < SSD_HEADDIM
    n_quads = SSD_HEADS // QUAD

    def chunk_constants(ck):
        rows = pl.ds(ck * T, T)
        dtla = dtla_ref[0, rows, :]
        dtla2 = jnp.where(lane >= 2 * SSD_HEADS, dtla * LOG2E, dtla)
        cs = jnp.dot(tri, dtla2, preferred_element_type=F32, precision=lax.Precision.HIGHEST)
        cs_t = cs.T
        dtla_t = dtla.T
        a_t = cs_t[la_c0:la_c0 + SSD_HEADS]
        a_end_t = cs_t[la_c0:la_c0 + SSD_HEADS, end_row:end_row + 1]
        dt_t = dtla_t[dt_c0:dt_c0 + SSD_HEADS]
        w_t = dt_t * jnp.exp2(a_end_t - a_t)
        adt_t = a_t - jnp.log2(dt_t)
        groups = []
        for g in range(SSD_GROUPS):
            b_g = xbc_ref[0, rows, D_SSD + g * SSD_STATE:D_SSD + (g + 1) * SSD_STATE]
            bt_g = b_g.astype(F32).T
            c_g = cb_g = None
            if with_y:
                c_g = xbc_ref[0, rows, D_SSD + D_BC + g * SSD_STATE:D_SSD + D_BC + (g + 1) * SSD_STATE]
                cb_g = _dot_nt(c_g, b_g)
            groups.append((bt_g, c_g, cb_g))
        return cs, w_t, adt_t, groups

    def scan_chunk(ck, constants):
        rows = pl.ds(ck * T, T)
        cs, w_t, adt_t, group_consts = constants

        def y_put(pc0, val):
            if fuse_out:
                y_scr[ck, :, pc0:pc0 + LANES] = val
            else:
                y_o[0, rows, pc0:pc0 + LANES] = val

        def y_get(pc0):
            if fuse_out:
                return y_scr[ck, :, pc0:pc0 + LANES]
            return y_o[0, rows, pc0:pc0 + LANES]

        boff = T if with_y else 0
        groups = []
        for g, (bt_g, c_g, cb_g) in enumerate(group_consts):
            yoff_g = None
            if with_y:
                gc0 = g * SSD_HPG * SSD_HEADDIM
                s_g = s_scr[:, gc0:gc0 + SSD_HPG * SSD_HEADDIM].astype(BF16)
                yoff_g = _dot(c_g, s_g)
            groups.append((bt_g, cb_g, yoff_g))

        def prep(q):
            g = q // (SSD_HPG // QUAD)
            bt_g, cb_g, yoff_g = groups[g]
            gc0 = g * SSD_HPG * SSD_HEADDIM
            h0 = q * QUAD
            c0 = h0 * SSD_HEADDIM
            slot = ck * n_quads + q
            for j in range(QUAD):
                rhs_scr[slot, pl.ds(j * T, T), j * SSD_HEADDIM:(j + 1) * SSD_HEADDIM] = (
                    xbc_ref[0, rows, c0 + j * SSD_HEADDIM:c0 + (j + 1) * SSD_HEADDIM])
            decays = []
            ar_even = None
            for j in range(QUAD):
                hh = h0 + j
                lhs_scr[slot, boff:boff + SSD_STATE, j * T:(j + 1) * T] = (
                    bt_g * w_t[hh:hh + 1]).astype(BF16)
                ar = jnp.broadcast_to(cs[:, la_c0 + hh:la_c0 + hh + 1], (T, LANES))
                if with_y:
                    seg = jnp.where(keep, ar - adt_t[hh:hh + 1], NEG_BIG)
                    lhs_scr[slot, 0:T, j * T:(j + 1) * T] = (cb_g * jnp.exp2(seg)).astype(BF16)
                if j % 2 == 0:
                    ar_even = ar
                    continue
                ar_pair = jnp.where(lane_pair, ar_even, ar)
                pc0 = c0 + (j // 2) * LANES
                if with_y:
                    yo = yoff_g[:, pc0 - gc0:pc0 - gc0 + LANES]
                    y_put(pc0, yo * jnp.exp2(ar_pair))
                decays.append(jnp.exp2(ar_pair[end_row:end_row + 1]))
            return decays

        def finish(q, decays):
            slot = ck * n_quads + q
            c0 = q * QUAD * SSD_HEADDIM
            res = _dot(lhs_scr[slot], rhs_scr[slot])
            for p in range(QUAD // 2):
                pc0 = c0 + p * LANES
                if with_y:
                    y_put(pc0, y_get(pc0) + res[:T, p * LANES:(p + 1) * LANES])
                s_scr[:, pc0:pc0 + LANES] = (s_scr[:, pc0:pc0 + LANES] * decays[p]
                                             + res[boff:, p * LANES:(p + 1) * LANES])

        for q in range(n_quads):
            finish(q, prep(q))

        if fuse_out:
            xs = xbc_ref[0, rows, 0:D_SSD].astype(F32)
            y = y_scr[ck] + yf_ref[0, rows, :] + dskip_ref[...] * xs
            gy = y * z_ref[0, rows, :].astype(F32)
            ms = jnp.mean(gy * gy, axis=-1, keepdims=True)
            y_o[0, rows, :] = (gy * lax.rsqrt(ms + EPS) * ng_ref[...]).astype(y_o.dtype)

    order = list(range(cps - 1, -1, -1) if reverse else range(cps))
    constants = {ck: chunk_constants(ck) for ck in order}
    for ck in order:
        scan_chunk(ck, constants[ck])

    @pl.when(i == n_i - 1)
    def _():
        hfin_o[0] = s_scr[...]


def _ssd_call(xbc, dtla, h0, reverse, with_y, fuse=None, cps=4):
    bsz, n_rows, _ = xbc.shape
    T = SSD_CHUNK
    cps = min(cps, n_rows // T)
    rows = cps * T
    ns = n_rows // rows
    cmap = (lambda b, i: (b, ns - 1 - i, 0)) if reverse else (lambda b, i: (b, i, 0))
    bmap = lambda b, i: (b, 0, 0)
    in_specs = [pl.BlockSpec((1, rows, D_XBC), cmap), pl.BlockSpec((1, rows, LANES), cmap)]
    args = [xbc, dtla]
    if h0 is not None:
        in_specs.append(pl.BlockSpec((1, SSD_STATE, D_SSD), bmap))
        args.append(h0)
    if fuse is not None:
        yf, zg, dskip, ng = fuse
        in_specs += [pl.BlockSpec((1, rows, D_SSD), cmap), pl.BlockSpec((1, rows, D_SSD), cmap),
                     _resident((1, D_SSD)), _resident((1, D_SSD))]
        args += [yf, zg, dskip, ng]
    out_shape, out_specs = [], []
    if with_y:
        out_shape.append(jax.ShapeDtypeStruct((bsz, n_rows, D_SSD), BF16 if fuse is not None else F32))
        out_specs.append(pl.BlockSpec((1, rows, D_SSD), cmap))
    out_shape.append(jax.ShapeDtypeStruct((bsz, SSD_STATE, D_SSD), F32))
    out_specs.append(pl.BlockSpec((1, SSD_STATE, D_SSD), bmap))
    scratch = [pltpu.VMEM((SSD_STATE, D_SSD), F32),
               pltpu.VMEM((cps * SSD_HEADS // QUAD, QUAD * T, QW), BF16),
               pltpu.VMEM((cps * SSD_HEADS // QUAD, (T if with_y else 0) + SSD_STATE, QUAD * T), BF16)]
    if fuse is not None:
        scratch.append(pltpu.VMEM((cps, T, D_SSD), F32))
    name = "ssd_" + ("rev" if reverse else "fwd") + ("_y" if with_y else "_state") + ("_out" if fuse else "")
    return pl.pallas_call(
        functools.partial(_ssd_kernel, reverse=reverse, has_h0=h0 is not None, with_y=with_y,
                          fuse_out=fuse is not None, cps=cps),
        grid=(bsz, ns),
        in_specs=in_specs, out_specs=out_specs, out_shape=out_shape,
        scratch_shapes=scratch,
        compiler_params=pltpu.CompilerParams(dimension_semantics=("arbitrary", "arbitrary"),
                                             vmem_limit_bytes=_vmem_limit(52 << 20)),
        name=name,
    )(*args)


ATT_STRIP = 16


def _attn_kernel(q_ref, kp_ref, kc_ref, kn_ref, vp_ref, vc_ref, vn_ref, kx_ref, vx_ref, sink_ref, o_ref,
                 s_scr, p_scr, bias_scr):
    blk = ATT_BLOCK
    n = pl.program_id(1)
    n_n = pl.num_programs(1)
    row = lax.broadcasted_iota(jnp.int32, (blk, blk), 0)
    col = lax.broadcasted_iota(jnp.int32, (blk, blk), 1)
    bias_scr[0] = jnp.where(jnp.logical_and(col >= row, n > 0), 0.0, NEG_BIG)
    bias_scr[1] = jnp.where(jnp.logical_and(col <= row, n < n_n - 1), 0.0, NEG_BIG)
    lane_kv = lax.broadcasted_iota(jnp.int32, (1, D_KV), 1) // ATT_HEADDIM
    kv_masks = [(lane_kv == j) for j in range(ATT_KV_HEADS)]
    kcat = jnp.concatenate([kp_ref[0], kc_ref[0], kn_ref[0], kx_ref[0]], axis=0)
    vcat = jnp.concatenate([vp_ref[0], vc_ref[0], vn_ref[0], vx_ref[0]], axis=0)
    nk = kcat.shape[0]
    for g in range(ATT_GROUP):
        qg = q_ref[0, :, g * D_KV:(g + 1) * D_KV]
        qs = jnp.concatenate([jnp.where(kv_masks[j], qg, jnp.zeros_like(qg))
                              for j in range(ATT_KV_HEADS)], axis=0)
        s_scr[g] = _dot_nt(qs, kcat)
    for g in range(ATT_GROUP):
        slot = g
        for st in range(ATT_KV_HEADS * blk // ATT_STRIP):
            r0 = st * ATT_STRIP
            i0 = r0 % blk
            sink = sink_ref[(r0 // blk) * ATT_GROUP + g] * LOG2E
            rows = pl.ds(r0, ATT_STRIP)
            parts = [s_scr[slot, rows, 0:blk] + bias_scr[0, pl.ds(i0, ATT_STRIP), :],
                     s_scr[slot, rows, blk:2 * blk],
                     s_scr[slot, rows, 2 * blk:3 * blk] + bias_scr[1, pl.ds(i0, ATT_STRIP), :]]
            parts += [s_scr[slot, rows, c0:c0 + LANES] for c0 in range(3 * blk, nk, LANES)]
            mel = parts[0]
            for t in parts[1:]:
                mel = jnp.maximum(mel, t)
            m = jnp.maximum(jnp.max(mel, axis=-1, keepdims=True), sink)
            ps = [jnp.exp2(t - m) for t in parts]
            tot = ps[0]
            for t in ps[1:]:
                tot = tot + t
            den = jnp.sum(tot, axis=-1, keepdims=True) + jnp.exp2(sink - m)
            inv = 1.0 / den
            for ci, t in enumerate(ps):
                p_scr[slot, rows, ci * LANES:(ci + 1) * LANES] = (t * inv).astype(BF16)
        o = _dot(p_scr[slot], vcat)
        og = jnp.zeros((blk, D_KV), F32)
        for j in range(ATT_KV_HEADS):
            og = jnp.where(kv_masks[j], o[j * blk:(j + 1) * blk], og)
        o_ref[0, :, g * D_KV:(g + 1) * D_KV] = og.astype(o_ref.dtype)


def _attn_call(q, k, v, kx, vx, sink):
    bsz, n_rows, _ = q.shape
    blk = ATT_BLOCK
    nb = n_rows // blk
    lc = kx.shape[1]
    cur = lambda b, n: (b, n, 0)
    prev = lambda b, n: (b, jnp.maximum(n - 1, 0), 0)
    nxt = lambda b, n: (b, jnp.minimum(n + 1, nb - 1), 0)
    ctx = lambda b, n: (b, 0, 0)
    kv_spec = lambda m: pl.BlockSpec((1, blk, D_KV), m)
    return pl.pallas_call(
        _attn_kernel,
        grid=(bsz, nb),
        in_specs=[pl.BlockSpec((1, blk, D_Q), cur),
                  kv_spec(prev), kv_spec(cur), kv_spec(nxt),
                  kv_spec(prev), kv_spec(cur), kv_spec(nxt),
                  pl.BlockSpec((1, lc, D_KV), ctx), pl.BlockSpec((1, lc, D_KV), ctx),
                  pl.BlockSpec(memory_space=pltpu.SMEM)],
        out_specs=pl.BlockSpec((1, blk, D_Q), cur),
        out_shape=jax.ShapeDtypeStruct((bsz, n_rows, D_Q), BF16),
        scratch_shapes=[pltpu.VMEM((ATT_GROUP, ATT_KV_HEADS * blk, 3 * blk + lc), F32),
                        pltpu.VMEM((ATT_GROUP, ATT_KV_HEADS * blk, 3 * blk + lc), BF16),
                        pltpu.VMEM((2, blk, blk), F32)],
        compiler_params=pltpu.CompilerParams(dimension_semantics=("arbitrary", "arbitrary"),
                                             vmem_limit_bytes=_vmem_limit(40 << 20)),
        name="attn",
    )(q, k, k, k, v, v, v, kx, vx, sink)


def _merge_kernel(y_ref, a_ref, gate_ref, x_ref, mod_ref, wos_ref, woa_ref, wout_ref, o_ref):
    gates = gate_ref[0].astype(F32)
    m = (gates[:, :D_MODEL] * _dot(y_ref[0], wos_ref[...])
         + gates[:, D_MODEL:] * _dot(a_ref[0], woa_ref[...]))
    o = _dot(m.astype(BF16), wout_ref[...])
    gt1 = mod_ref[0, :, 2 * D_MODEL:3 * D_MODEL]
    o_ref[0] = x_ref[0] + gt1 * o


def _merge_call(yn, att, gates, x, modv, w_os, w_oa, w_out, tm):
    bsz, n_rows, _ = x.shape
    row_map = lambda b, i: (b, i, 0)
    return pl.pallas_call(
        _merge_kernel,
        grid=(bsz, n_rows // tm),
        in_specs=[pl.BlockSpec((1, tm, D_SSD), row_map), pl.BlockSpec((1, tm, D_Q), row_map),
                  pl.BlockSpec((1, tm, 2 * D_MODEL), row_map), pl.BlockSpec((1, tm, D_MODEL), row_map),
                  pl.BlockSpec((1, 1, N_MOD * D_MODEL), lambda b, i: (b, 0, 0)),
                  _resident((D_SSD, D_MODEL)), _resident((D_Q, D_MODEL)), _resident((D_MODEL, D_MODEL))],
        out_specs=pl.BlockSpec((1, tm, D_MODEL), row_map),
        out_shape=jax.ShapeDtypeStruct((bsz, n_rows, D_MODEL), F32),
        compiler_params=pltpu.CompilerParams(dimension_semantics=("arbitrary", "arbitrary"),
                                             vmem_limit_bytes=_vmem_limit(48 << 20)),
        name="merge",
    )(yn, att, gates, x, modv, w_os, w_oa, w_out)


FF_CHUNK = 256


def _ffn_kernel(x_ref, xp_ref, xn_ref, mod_ref, g_ref, wup_ref, cw_ref, cb_ref, wdn_ref, fg_ref,
                o_ref, ua_scr, ub_scr, act_scr, *, tm):
    i = pl.program_id(1)
    n_i = pl.num_programs(1)
    g = g_ref[...]
    shift = mod_ref[0, :, 3 * D_MODEL:4 * D_MODEL]
    scale = mod_ref[0, :, 4 * D_MODEL:5 * D_MODEL]
    gt2 = mod_ref[0, :, 5 * D_MODEL:6 * D_MODEL]
    x = x_ref[0]
    h = _modulated_norm(x, g, shift, scale)
    hp = _modulated_norm(xp_ref[0], g, shift, scale) * (i > 0).astype(F32)
    hn = _modulated_norm(xn_ref[0], g, shift, scale) * (i < n_i - 1).astype(F32)
    h_ext = jnp.concatenate([hp, h, hn], axis=0).astype(BF16)
    for c in range(D_FF // FF_CHUNK):
        c0 = c * FF_CHUNK
        ua_scr[c] = _dot(h_ext, wup_ref[:, c0:c0 + FF_CHUNK])
        ub_scr[c] = _dot(h_ext, wup_ref[:, D_FF + c0:D_FF + c0 + FF_CHUNK])
    for c in range(D_FF // FF_CHUNK):
        c0 = c * FF_CHUNK
        a = (ua_scr[c, pl.ds(HALO - 1, tm), :] * cw_ref[0:1, c0:c0 + FF_CHUNK]
             + ua_scr[c, pl.ds(HALO, tm), :] * cw_ref[1:2, c0:c0 + FF_CHUNK]
             + ua_scr[c, pl.ds(HALO + 1, tm), :] * cw_ref[2:3, c0:c0 + FF_CHUNK]
             + cb_ref[:, c0:c0 + FF_CHUNK])
        b0 = D_FF + c0
        b = (ub_scr[c, pl.ds(HALO - 1, tm), :] * cw_ref[0:1, b0:b0 + FF_CHUNK]
             + ub_scr[c, pl.ds(HALO, tm), :] * cw_ref[1:2, b0:b0 + FF_CHUNK]
             + ub_scr[c, pl.ds(HALO + 1, tm), :] * cw_ref[2:3, b0:b0 + FF_CHUNK]
             + cb_ref[:, b0:b0 + FF_CHUNK])
        act_scr[:, c0:c0 + FF_CHUNK] = (_silu(a) * b).astype(BF16)
    f = _dot(act_scr[...], wdn_ref[...])
    x2 = x + gt2 * f
    ms = jnp.mean(x2 * x2, axis=-1, keepdims=True)
    o_ref[0] = x2 * lax.rsqrt(ms + EPS) * fg_ref[...]


def _ffn_call(x1, modv, norm_g, w_up, conv_w, conv_b, w_down, final_g, tm):
    bsz, n_rows, _ = x1.shape
    prev_map, next_map = _halo_maps(tm, n_rows)
    row_map = lambda b, i: (b, i, 0)
    return pl.pallas_call(
        functools.partial(_ffn_kernel, tm=tm),
        grid=(bsz, n_rows // tm),
        in_specs=[pl.BlockSpec((1, tm, D_MODEL), row_map),
                  pl.BlockSpec((1, HALO, D_MODEL), prev_map),
                  pl.BlockSpec((1, HALO, D_MODEL), next_map),
                  pl.BlockSpec((1, 1, N_MOD * D_MODEL), lambda b, i: (b, 0, 0)),
                  _resident((1, D_MODEL)),
                  _resident((D_MODEL, 2 * D_FF)), _resident((3, 2 * D_FF)), _resident((1, 2 * D_FF)),
                  _resident((D_FF, D_MODEL)), _resident((1, D_MODEL))],
        out_specs=pl.BlockSpec((1, tm, D_MODEL), row_map),
        out_shape=jax.ShapeDtypeStruct((bsz, n_rows, D_MODEL), F32),
        scratch_shapes=[pltpu.VMEM((D_FF // FF_CHUNK, tm + 2 * HALO, FF_CHUNK), F32),
                        pltpu.VMEM((D_FF // FF_CHUNK, tm + 2 * HALO, FF_CHUNK), F32),
                        pltpu.VMEM((tm, D_FF), BF16)],
        compiler_params=pltpu.CompilerParams(dimension_semantics=("arbitrary", "arbitrary"),
                                             vmem_limit_bytes=_vmem_limit(48 << 20)),
        name="ffn",
    )(x1, x1, x1, modv, norm_g, w_up, conv_w, conv_b, w_down, final_g)


def _rope_tables(n_rows):
    half = ATT_HEADDIM // 2
    inv = ROPE_BASE ** (-jnp.arange(0, half, 2, dtype=F32) / half)
    pos = jnp.arange(n_rows)
    rowp = (pos // GRID_W).astype(F32)[:, None] * inv[None]
    colp = (pos % GRID_W).astype(F32)[:, None] * inv[None]
    cos = jnp.concatenate([jnp.cos(rowp), jnp.cos(rowp), jnp.cos(colp), jnp.cos(colp)], axis=-1)
    sin = jnp.concatenate([-jnp.sin(rowp), jnp.sin(rowp), -jnp.sin(colp), jnp.sin(colp)], axis=-1)
    reps = LANES // ATT_HEADDIM
    return jnp.tile(cos, (1, reps)), jnp.tile(sin, (1, reps))


def kernel(x, c, ctx, c_ctx, w_mod, b_mod, norm1_g, norm2_g, w_in, ssd_conv_w, ssd_conv_b, ssd_dt_bias,
           ssd_a_log, ssd_d, ssd_norm_g, w_o_ssd, w_o_att, att_sink, w_out, w_up, ffn_conv_w, ffn_conv_b,
           w_down, final_g):
    depth = w_mod.shape[0]
    assert depth == 1, "single-layer block"
    l = 0
    bsz, n_lat, _ = x.shape
    assert bsz + 1 <= SUBLANES

    cc = jnp.concatenate([c, c_ctx[None], jnp.zeros((SUBLANES - bsz - 1, D_MODEL), F32)], axis=0)
    mod_all = _mod_call(cc, w_mod[l], b_mod[l][None])
    mod_lat = mod_all[:bsz, None, :]
    mod_ctx = mod_all[bsz:bsz + 1, None, :]

    wi = w_in[l]
    w_q = (wi[:, Q0:K0].reshape(D_MODEL, ATT_KV_HEADS, ATT_GROUP, ATT_HEADDIM)
           .transpose(0, 2, 1, 3).reshape(D_MODEL, D_Q))
    w = jnp.concatenate([wi[:, XBC0:DT0], wi[:, DT0:Q0], wi[:, DT0:Q0], wi[:, K0:G0],
                         wi[:, Z0:XBC0], w_q, wi[:, G0:D_IN]], axis=1).astype(BF16)
    dtb = ssd_dt_bias[l].reshape(1, 2 * SSD_HEADS)
    a_mult = -jnp.exp(ssd_a_log[l].reshape(1, 2 * SSD_HEADS))
    consts = {"conv_w": ssd_conv_w[l], "conv_b": ssd_conv_b[l][None],
              "dt_bias": jnp.concatenate([dtb, dtb], axis=1),
              "a_mult": jnp.concatenate([jnp.ones_like(a_mult), a_mult], axis=1)}
    rope = _rope_tables(n_lat)
    g1 = norm1_g[l][None]

    xbc_c, dtla_c, k_c, v_c = _inproj_call(ctx, mod_ctx, False, g1, w, consts, None, ctx.shape[1], False)
    xbc_l, dtla_l, k_l, v_l, zg, q_l, gates = _inproj_call(x, mod_lat, True, g1, w, consts, rope, 256, True)

    (hf_c,) = _ssd_call(xbc_c, dtla_c, None, False, False)
    (hb_c,) = _ssd_call(xbc_c, dtla_c, None, True, False)
    yf, _ = _ssd_call(xbc_l, dtla_l, hf_c, False, True)
    dskip = jnp.repeat(ssd_d[l], SSD_HEADDIM)[None]
    yn, _ = _ssd_call(xbc_l, dtla_l, hb_c, True, True, fuse=(yf, zg, dskip, ssd_norm_g[l][None]))

    sink = att_sink[l].astype(F32)
    att = _attn_call(q_l, k_l, v_l, k_c, v_c, sink)

    w_oa = (w_o_att[l].reshape(ATT_KV_HEADS, ATT_GROUP, ATT_HEADDIM, D_MODEL)
            .transpose(1, 0, 2, 3).reshape(D_Q, D_MODEL))
    x1 = _merge_call(yn, att, gates, x, mod_lat, w_o_ssd[l].astype(BF16),
                     w_oa.astype(BF16), w_out[l].astype(BF16), 512)
    return _ffn_call(x1, mod_lat, norm2_g[l][None], w_up[l].astype(BF16), ffn_conv_w[l],
                     ffn_conv_b[l][None], w_down[l].astype(BF16), final_g[None], 512)
```

```python
import functools
import math

import jax
import jax.numpy as jnp
import numpy as np
from jax import lax
from jax.experimental import pallas as pl
from jax.experimental.pallas import tpu as pltpu

F32 = jnp.float32
BF16 = jnp.bfloat16

D_MODEL = 1024
GRID_W = 64
EPS = 1e-6
N_MOD = 6
D_SSD = 2048
SSD_HEADDIM = 64
SSD_HEADS = 32
SSD_GROUPS = 4
SSD_HPG = 8
SSD_STATE = 128
SSD_CHUNK = 128
D_BC = SSD_GROUPS * SSD_STATE
D_XBC = D_SSD + 2 * D_BC
ATT_HEADS = 16
ATT_KV_HEADS = 4
ATT_GROUP = 4
ATT_HEADDIM = 64
ATT_BLOCK = 128
D_Q = ATT_HEADS * ATT_HEADDIM
D_KV = ATT_KV_HEADS * ATT_HEADDIM
ROPE_BASE = 10000.0
D_FF = 2816
Z0 = 0
XBC0 = Z0 + D_SSD
DT0 = XBC0 + D_XBC
Q0 = DT0 + 2 * SSD_HEADS
K0 = Q0 + D_Q
V0 = K0 + D_KV
G0 = V0 + D_KV
D_IN = G0 + 2 * D_MODEL

V7X_VMEM_BYTES = 64 * 1024 * 1024
LANES = 128
SUBLANES = 8
HALO = SUBLANES
NEG_BIG = -1e30
LOG2E = 1.4426950408889634


def _vmem_limit(nbytes):
    return int(min(nbytes, V7X_VMEM_BYTES - 8 * 1024 * 1024))


def _resident(shape):
    nd = len(shape)
    return pl.BlockSpec(shape, lambda *_: (0,) * nd, pipeline_mode=pl.Buffered(1))


def _sigmoid(x):
    return 1.0 / (1.0 + jnp.exp2(x * (-LOG2E)))


def _silu(x):
    return x * _sigmoid(x)


def _dot(a, b):
    return jnp.dot(a, b, preferred_element_type=F32)


def _dot_nt(a, b):
    return lax.dot_general(a, b, (((1,), (1,)), ((), ())), preferred_element_type=F32)


def _modulated_norm(x, g, shift, scale):
    ms = jnp.mean(x * x, axis=-1, keepdims=True)
    return x * lax.rsqrt(ms + EPS) * g * (1.0 + scale) + shift


def _mod_kernel(c_ref, w_ref, b_ref, o_ref):
    s = _silu(c_ref[...])
    o_ref[...] = _dot(s.astype(BF16), w_ref[...].astype(BF16)) + b_ref[...]


def _mod_call(cc, w_mod, b_mod):
    n = w_mod.shape[1]
    tn = 1536
    return pl.pallas_call(
        _mod_kernel,
        grid=(n // tn,),
        in_specs=[pl.BlockSpec((SUBLANES, D_MODEL), lambda j: (0, 0)),
                  pl.BlockSpec((D_MODEL, tn), lambda j: (0, j)),
                  pl.BlockSpec((1, tn), lambda j: (0, j))],
        out_specs=pl.BlockSpec((SUBLANES, tn), lambda j: (0, j)),
        out_shape=jax.ShapeDtypeStruct((SUBLANES, n), F32),
        compiler_params=pltpu.CompilerParams(dimension_semantics=("arbitrary",),
                                             vmem_limit_bytes=_vmem_limit(40 << 20)),
        name="mod",
    )(cc, w_mod, b_mod)


def _halo_maps(tm, n_rows):
    hb = tm // HALO
    last = n_rows // HALO - 1
    prev_map = lambda b, i: (b, jnp.maximum(i * hb - 1, 0), 0)
    next_map = lambda b, i: (b, jnp.minimum((i + 1) * hb, last), 0)
    return prev_map, next_map


def _conv3_rows(u, tm, w, b):
    n = u.shape[0]
    before = pltpu.roll(u, 1, 0)[0:tm]
    after = pltpu.roll(u, n - 1, 0)[0:tm]
    return before * w[0:1] + u[0:tm] * w[1:2] + after * w[2:3] + b


def _store_normed_rows(h_scr, x_ref, xp_ref, xn_ref, g, shift, scale, tm):
    i = pl.program_id(1)
    n_i = pl.num_programs(1)
    h = _modulated_norm(x_ref[0], g, shift, scale)
    hp = _modulated_norm(xp_ref[0], g, shift, scale) * (i > 0).astype(F32)
    hn = _modulated_norm(xn_ref[0], g, shift, scale) * (i < n_i - 1).astype(F32)
    h_scr[0:tm, :] = h.astype(BF16)
    h_scr[tm:tm + 2 * HALO, :] = jnp.concatenate([hn, hp], axis=0).astype(BF16)


def _rope(t, cos, sin_signed, n_tiles):
    width = t.shape[1]
    lane = lax.broadcasted_iota(jnp.int32, t.shape, 1)
    first = (lane % 32) < 16
    partner = jnp.where(first, pltpu.roll(t, width - 16, 1), pltpu.roll(t, 16, 1))
    cos_t = jnp.concatenate([cos] * n_tiles, axis=1)
    sin_t = jnp.concatenate([sin_signed] * n_tiles, axis=1)
    return t * cos_t + partner * sin_t


W_XBC0 = 0
W_DT0 = W_XBC0 + D_XBC
W_K0 = W_DT0 + LANES
W_V0 = W_K0 + D_KV
W_CTX_END = W_V0 + D_KV
W_Z0 = W_CTX_END
W_Q0 = W_Z0 + D_SSD
W_G0 = W_Q0 + D_Q
W_END = W_G0 + 2 * D_MODEL


def _inproj_kernel(*refs, tm, latent):
    if latent:
        (x_ref, xp_ref, xn_ref, mod_ref, g_ref, w_ref, cw_ref, cb_ref, dtb_ref, am_ref, cos_ref, sin_ref,
         xbc_o, dtla_o, k_o, v_o, z_o, q_o, gate_o, u_scr, p_scr, h_scr) = refs
    else:
        (x_ref, xp_ref, xn_ref, mod_ref, g_ref, w_ref, cw_ref, cb_ref, dtb_ref, am_ref,
         xbc_o, dtla_o, k_o, v_o, u_scr, p_scr, h_scr) = refs
    i = pl.program_id(1)
    n_i = pl.num_programs(1)

    @pl.when(i >= 0)
    def _():
        _store_normed_rows(h_scr, x_ref, xp_ref, xn_ref, g_ref[...], mod_ref[0, :, 0:D_MODEL],
                           mod_ref[0, :, D_MODEL:2 * D_MODEL], tm)

    u_scr[...] = _dot(h_scr[...], w_ref[:, W_XBC0:W_DT0])
    p_scr[...] = _dot(h_scr[0:tm, :], w_ref[:, W_DT0:(W_END if latent else W_CTX_END)])
    p0 = W_DT0

    cchunk = 512
    for c0 in range(0, D_XBC, cchunk):
        cols = slice(c0, c0 + cchunk)
        n = tm + 2 * HALO
        before = jnp.concatenate([u_scr[n - 1:n, cols], u_scr[0:tm - 1, cols]], axis=0)
        after = jnp.concatenate([u_scr[1:tm, cols], u_scr[tm:tm + 1, cols]], axis=0)
        y = (before * cw_ref[0:1, cols] + u_scr[0:tm, cols] * cw_ref[1:2, cols]
             + after * cw_ref[2:3, cols] + cb_ref[:, cols])
        xbc_o[0, :, cols] = _silu(y).astype(BF16)

    raw = p_scr[:, W_DT0 - p0:W_K0 - p0] + dtb_ref[...]
    dt = jnp.maximum(raw, 0.0) + jnp.log1p(jnp.exp(-jnp.abs(raw)))
    lane = lax.broadcasted_iota(jnp.int32, dt.shape, 1)
    dtla_o[0] = jnp.where(lane < 2 * SSD_HEADS, dt, dt * am_ref[...])

    kf = p_scr[:, W_K0 - p0:W_V0 - p0]
    if latent:
        cos = cos_ref[...]
        sin = sin_ref[...]
        kf = _rope(kf, cos, sin, D_KV // LANES)
    k_o[0] = kf.astype(BF16)
    v_o[0] = p_scr[:, W_V0 - p0:W_CTX_END - p0].astype(BF16)
    if latent:
        z_o[0] = _silu(p_scr[:, W_Z0 - p0:W_Q0 - p0]).astype(BF16)
        qf = _rope(p_scr[:, W_Q0 - p0:W_G0 - p0], cos, sin, D_Q // LANES)
        q_o[0] = (qf * (ATT_HEADDIM ** -0.5 * LOG2E)).astype(BF16)
        gate_o[0] = _sigmoid(p_scr[:, W_G0 - p0:W_END - p0]).astype(BF16)


def _inproj_call(x, modv, per_batch_mod, norm_g, w_all, consts, rope, tm, latent):
    bsz, n_rows, _ = x.shape
    prev_map, next_map = _halo_maps(tm, n_rows)
    row_map = lambda b, i: (b, i, 0)
    mod_map = (lambda b, i: (b, 0, 0)) if per_batch_mod else (lambda b, i: (0, 0, 0))
    in_specs = [pl.BlockSpec((1, tm, D_MODEL), row_map),
                pl.BlockSpec((1, HALO, D_MODEL), prev_map),
                pl.BlockSpec((1, HALO, D_MODEL), next_map),
                pl.BlockSpec((1, 1, N_MOD * D_MODEL), mod_map),
                _resident((1, D_MODEL)),
                _resident((D_MODEL, W_END if latent else W_CTX_END)),
                _resident((3, D_XBC)), _resident((1, D_XBC)),
                _resident((1, LANES)), _resident((1, LANES))]
    args = [x, x, x, modv, norm_g, w_all, consts["conv_w"], consts["conv_b"],
            consts["dt_bias"], consts["a_mult"]]
    out_shape = [jax.ShapeDtypeStruct((bsz, n_rows, D_XBC), BF16),
                 jax.ShapeDtypeStruct((bsz, n_rows, LANES), F32),
                 jax.ShapeDtypeStruct((bsz, n_rows, D_KV), BF16),
                 jax.ShapeDtypeStruct((bsz, n_rows, D_KV), BF16)]
    out_specs = [pl.BlockSpec((1, tm, D_XBC), row_map), pl.BlockSpec((1, tm, LANES), row_map),
                 pl.BlockSpec((1, tm, D_KV), row_map), pl.BlockSpec((1, tm, D_KV), row_map)]
    if latent:
        in_specs += [pl.BlockSpec((tm, LANES), lambda b, i: (i, 0)),
                     pl.BlockSpec((tm, LANES), lambda b, i: (i, 0))]
        args += [rope[0], rope[1]]
        out_shape += [jax.ShapeDtypeStruct((bsz, n_rows, D_SSD), BF16),
                      jax.ShapeDtypeStruct((bsz, n_rows, D_Q), BF16),
                      jax.ShapeDtypeStruct((bsz, n_rows, 2 * D_MODEL), BF16)]
        out_specs += [pl.BlockSpec((1, tm, D_SSD), row_map), pl.BlockSpec((1, tm, D_Q), row_map),
                      pl.BlockSpec((1, tm, 2 * D_MODEL), row_map)]
    return pl.pallas_call(
        functools.partial(_inproj_kernel, tm=tm, latent=latent),
        grid=(bsz, n_rows // tm),
        in_specs=in_specs, out_specs=out_specs, out_shape=out_shape,
        scratch_shapes=[pltpu.VMEM((tm + 2 * HALO, D_XBC), F32),
                        pltpu.VMEM((tm, (W_END if latent else W_CTX_END) - W_DT0), F32),
                        pltpu.VMEM((tm + 2 * HALO, D_MODEL), BF16)],
        compiler_params=pltpu.CompilerParams(dimension_semantics=("arbitrary", "arbitrary"),
                                             vmem_limit_bytes=_vmem_limit(56 << 20)),
        name="inproj_lat" if latent else "inproj_ctx",
    )(*args)


QUAD = 4
QW = QUAD * SSD_HEADDIM


def _ssd_kernel(*refs, reverse, has_h0, with_y, fuse_out, cps):
    T = SSD_CHUNK
    it = iter(refs)
    xbc_ref = next(it)
    dtla_ref = next(it)
    h0_ref = next(it) if has_h0 else None
    if fuse_out:
        yf_ref, z_ref, dskip_ref, ng_ref = next(it), next(it), next(it), next(it)
    y_o = next(it) if with_y else None
    hfin_o = next(it)
    s_scr = next(it)
    rhs_scr = next(it)
    lhs_scr = next(it)
    y_scr = next(it) if fuse_out else None

    i = pl.program_id(1)
    n_i = pl.num_programs(1)

    @pl.when(i == 0)
    def _():
        if has_h0:
            s_scr[...] = h0_ref[0]
        else:
            s_scr[...] = jnp.zeros_like(s_scr)
        rhs_scr[...] = jnp.zeros_like(rhs_scr)

    d = 1 if reverse else 0
    dt_c0 = d * SSD_HEADS
    la_c0 = 2 * SSD_HEADS + d * SSD_HEADS
    lane = lax.broadcasted_iota(jnp.int32, (T, LANES), 1)
    row = lax.broadcasted_iota(jnp.int32, (T, T), 0)
    col = lax.broadcasted_iota(jnp.int32, (T, T), 1)
    keep = (col >= row) if reverse else (col <= row)
    tri = keep.astype(F32)
    end_row = 0 if reverse else T - 1
    lane_pair = lax.broadcasted_iota(jnp.int32, (1, LANES), 1) < SSD_HEADDIM
    n_quads = SSD_HEADS // QUAD

    def chunk_constants(ck):
        rows = pl.ds(ck * T, T)
        dtla = dtla_ref[0, rows, :]
        dtla2 = jnp.where(lane >= 2 * SSD_HEADS, dtla * LOG2E, dtla)
        cs = jnp.dot(tri, dtla2, preferred_element_type=F32, precision=lax.Precision.HIGHEST)
        cs_t = cs.T
        dtla_t = dtla.T
        a_t = cs_t[la_c0:la_c0 + SSD_HEADS]
        a_end_t = cs_t[la_c0:la_c0 + SSD_HEADS, end_row:end_row + 1]
        dt_t = dtla_t[dt_c0:dt_c0 + SSD_HEADS]
        w_t = dt_t * jnp.exp2(a_end_t - a_t)
        adt_t = a_t - jnp.log2(dt_t)
        groups = []
        for g in range(SSD_GROUPS):
            b_g = xbc_ref[0, rows, D_SSD + g * SSD_STATE:D_SSD + (g + 1) * SSD_STATE]
            bt_g = b_g.astype(F32).T
            c_g = cb_g = None
            if with_y:
                c_g = xbc_ref[0, rows, D_SSD + D_BC + g * SSD_STATE:D_SSD + D_BC + (g + 1) * SSD_STATE]
                cb_g = _dot_nt(c_g, b_g)
            groups.append((bt_g, c_g, cb_g))
        return cs, w_t, adt_t, groups

    def scan_chunk(ck, constants):
        rows = pl.ds(ck * T, T)
        cs, w_t, adt_t, group_consts = constants

        def y_put(pc0, val):
            if fuse_out:
                y_scr[ck, :, pc0:pc0 + LANES] = val
            else:
                y_o[0, rows, pc0:pc0 + LANES] = val

        def y_get(pc0):
            if fuse_out:
                return y_scr[ck, :, pc0:pc0 + LANES]
            return y_o[0, rows, pc0:pc0 + LANES]

        boff = T if with_y else 0
        groups = []
        for g, (bt_g, c_g, cb_g) in enumerate(group_consts):
            yoff_g = None
            if with_y:
                gc0 = g * SSD_HPG * SSD_HEADDIM
                s_g = s_scr[:, gc0:gc0 + SSD_HPG * SSD_HEADDIM].astype(BF16)
                yoff_g = _dot(c_g, s_g)
            groups.append((bt_g, cb_g, yoff_g))

        def prep(q):
            g = q // (SSD_HPG // QUAD)
            bt_g, cb_g, yoff_g = groups[g]
            gc0 = g * SSD_HPG * SSD_HEADDIM
            h0 = q * QUAD
            c0 = h0 * SSD_HEADDIM
            slot = ck * n_quads + q
            for j in range(QUAD):
                rhs_scr[slot, pl.ds(j * T, T), j * SSD_HEADDIM:(j + 1) * SSD_HEADDIM] = (
                    xbc_ref[0, rows, c0 + j * SSD_HEADDIM:c0 + (j + 1) * SSD_HEADDIM])
            decays = []
            ar_even = None
            for j in range(QUAD):
                hh = h0 + j
                lhs_scr[slot, boff:boff + SSD_STATE, j * T:(j + 1) * T] = (
                    bt_g * w_t[hh:hh + 1]).astype(BF16)
                ar = jnp.broadcast_to(cs[:, la_c0 + hh:la_c0 + hh + 1], (T, LANES))
                if with_y:
                    seg = jnp.where(keep, ar - adt_t[hh:hh + 1], NEG_BIG)
                    lhs_scr[slot, 0:T, j * T:(j + 1) * T] = (cb_g * jnp.exp2(seg)).astype(BF16)
                if j % 2 == 0:
                    ar_even = ar
                    continue
                ar_pair = jnp.where(lane_pair, ar_even, ar)
                pc0 = c0 + (j // 2) * LANES
                if with_y:
                    yo = yoff_g[:, pc0 - gc0:pc0 - gc0 + LANES]
                    y_put(pc0, yo * jnp.exp2(ar_pair))
                decays.append(jnp.exp2(ar_pair[end_row:end_row + 1]))
            return decays

        def finish(q, decays):
            slot = ck * n_quads + q
            c0 = q * QUAD * SSD_HEADDIM
            res = _dot(lhs_scr[slot], rhs_scr[slot])
            for p in range(QUAD // 2):
                pc0 = c0 + p * LANES
                if with_y:
                    y_put(pc0, y_get(pc0) + res[:T, p * LANES:(p + 1) * LANES])
                s_scr[:, pc0:pc0 + LANES] = (s_scr[:, pc0:pc0 + LANES] * decays[p]
                                             + res[boff:, p * LANES:(p + 1) * LANES])

        for q in range(n_quads):
            finish(q, prep(q))

        if fuse_out:
            xs = xbc_ref[0, rows, 0:D_SSD].astype(F32)
            y = y_scr[ck] + yf_ref[0, rows, :] + dskip_ref[...] * xs
            gy = y * z_ref[0, rows, :].astype(F32)
            ms = jnp.mean(gy * gy, axis=-1, keepdims=True)
            y_o[0, rows, :] = (gy * lax.rsqrt(ms + EPS) * ng_ref[...]).astype(y_o.dtype)

    order = list(range(cps - 1, -1, -1) if reverse else range(cps))
    constants = {ck: chunk_constants(ck) for ck in order}
    for ck in order:
        scan_chunk(ck, constants[ck])

    @pl.when(i == n_i - 1)
    def _():
        hfin_o[0] = s_scr[...]


def _ssd_call(xbc, dtla, h0, reverse, with_y, fuse=None, cps=4):
    bsz, n_rows, _ = xbc.shape
    T = SSD_CHUNK
    cps = min(cps, n_rows // T)
    rows = cps * T
    ns = n_rows // rows
    cmap = (lambda b, i: (b, ns - 1 - i, 0)) if reverse else (lambda b, i: (b, i, 0))
    bmap = lambda b, i: (b, 0, 0)
    in_specs = [pl.BlockSpec((1, rows, D_XBC), cmap), pl.BlockSpec((1, rows, LANES), cmap)]
    args = [xbc, dtla]
    if h0 is not None:
        in_specs.append(pl.BlockSpec((1, SSD_STATE, D_SSD), bmap))
        args.append(h0)
    if fuse is not None:
        yf, zg, dskip, ng = fuse
        in_specs += [pl.BlockSpec((1, rows, D_SSD), cmap), pl.BlockSpec((1, rows, D_SSD), cmap),
                     _resident((1, D_SSD)), _resident((1, D_SSD))]
        args += [yf, zg, dskip, ng]
    out_shape, out_specs = [], []
    if with_y:
        out_shape.append(jax.ShapeDtypeStruct((bsz, n_rows, D_SSD), BF16 if fuse is not None else F32))
        out_specs.append(pl.BlockSpec((1, rows, D_SSD), cmap))
    out_shape.append(jax.ShapeDtypeStruct((bsz, SSD_STATE, D_SSD), F32))
    out_specs.append(pl.BlockSpec((1, SSD_STATE, D_SSD), bmap))
    scratch = [pltpu.VMEM((SSD_STATE, D_SSD), F32),
               pltpu.VMEM((cps * SSD_HEADS // QUAD, QUAD * T, QW), BF16),
               pltpu.VMEM((cps * SSD_HEADS // QUAD, (T if with_y else 0) + SSD_STATE, QUAD * T), BF16)]
    if fuse is not None:
        scratch.append(pltpu.VMEM((cps, T, D_SSD), F32))
    name = "ssd_" + ("rev" if reverse else "fwd") + ("_y" if with_y else "_state") + ("_out" if fuse else "")
    return pl.pallas_call(
        functools.partial(_ssd_kernel, reverse=reverse, has_h0=h0 is not None, with_y=with_y,
                          fuse_out=fuse is not None, cps=cps),
        grid=(bsz, ns),
        in_specs=in_specs, out_specs=out_specs, out_shape=out_shape,
        scratch_shapes=scratch,
        compiler_params=pltpu.CompilerParams(dimension_semantics=("arbitrary", "arbitrary"),
                                             vmem_limit_bytes=_vmem_limit(52 << 20)),
        name=name,
    )(*args)


ATT_STRIP = 16
ATT_QB = 2


def _attn_kernel(q_ref, kp_ref, kc_ref, kn_ref, vp_ref, vc_ref, vn_ref, kx_ref, vx_ref, sink_ref, o_ref,
                 s_scr, p_scr, bias_scr):
    blk = ATT_BLOCK
    n = pl.program_id(1)
    n_n = pl.num_programs(1)
    row = lax.broadcasted_iota(jnp.int32, (blk, blk), 0)
    col = lax.broadcasted_iota(jnp.int32, (blk, blk), 1)
    lane_kv = lax.broadcasted_iota(jnp.int32, (1, D_KV), 1) // ATT_HEADDIM
    kv_masks = [(lane_kv == j) for j in range(ATT_KV_HEADS)]
    kwin = [kp_ref[0]] + [kc_ref[0, sb * blk:(sb + 1) * blk, :] for sb in range(ATT_QB)] + [kn_ref[0]]
    vwin = [vp_ref[0]] + [vc_ref[0, sb * blk:(sb + 1) * blk, :] for sb in range(ATT_QB)] + [vn_ref[0]]
    vcats = []
    for sb in range(ATT_QB):
        prev_ok = (col >= row) if sb > 0 else jnp.logical_and(col >= row, n > 0)
        next_ok = (col <= row) if sb < ATT_QB - 1 else jnp.logical_and(col <= row, n < n_n - 1)
        bias_scr[2 * sb] = jnp.where(prev_ok, 0.0, NEG_BIG)
        bias_scr[2 * sb + 1] = jnp.where(next_ok, 0.0, NEG_BIG)
        kcat = jnp.concatenate(kwin[sb:sb + 3] + [kx_ref[0]], axis=0)
        vcats.append(jnp.concatenate(vwin[sb:sb + 3] + [vx_ref[0]], axis=0))
        nk = kcat.shape[0]
        for g in range(ATT_GROUP):
            qg = q_ref[0, sb * blk:(sb + 1) * blk, g * D_KV:(g + 1) * D_KV]
            qs = jnp.concatenate([jnp.where(kv_masks[j], qg, jnp.zeros_like(qg))
                                  for j in range(ATT_KV_HEADS)], axis=0)
            s_scr[sb * ATT_GROUP + g] = _dot_nt(qs, kcat)
    for sb, g in [(sb, g) for sb in range(ATT_QB) for g in range(ATT_GROUP)]:
        slot = sb * ATT_GROUP + g
        vcat = vcats[sb]
        for st in range(ATT_KV_HEADS * blk // ATT_STRIP):
            r0 = st * ATT_STRIP
            i0 = r0 % blk
            sink = sink_ref[(r0 // blk) * ATT_GROUP + g] * LOG2E
            rows = pl.ds(r0, ATT_STRIP)
            parts = [s_scr[slot, rows, 0:blk] + bias_scr[2 * sb, pl.ds(i0, ATT_STRIP), :],
                     s_scr[slot, rows, blk:2 * blk],
                     s_scr[slot, rows, 2 * blk:3 * blk] + bias_scr[2 * sb + 1, pl.ds(i0, ATT_STRIP), :]]
            parts += [s_scr[slot, rows, c0:c0 + LANES] for c0 in range(3 * blk, nk, LANES)]
            mel = parts[0]
            for t in parts[1:]:
                mel = jnp.maximum(mel, t)
            m = jnp.maximum(jnp.max(mel, axis=-1, keepdims=True), sink)
            ps = [jnp.exp2(t - m) for t in parts]
            tot = ps[0]
            for t in ps[1:]:
                tot = tot + t
            den = jnp.sum(tot, axis=-1, keepdims=True) + jnp.exp2(sink - m)
            inv = 1.0 / den
            for ci, t in enumerate(ps):
                p_scr[slot, rows, ci * LANES:(ci + 1) * LANES] = (t * inv).astype(BF16)
        o = _dot(p_scr[slot], vcat)
        og = jnp.zeros((blk, D_KV), F32)
        for j in range(ATT_KV_HEADS):
            og = jnp.where(kv_masks[j], o[j * blk:(j + 1) * blk], og)
        o_ref[0, sb * blk:(sb + 1) * blk, g * D_KV:(g + 1) * D_KV] = og.astype(o_ref.dtype)


def _attn_call(q, k, v, kx, vx, sink):
    bsz, n_rows, _ = q.shape
    blk = ATT_BLOCK
    nb = n_rows // blk
    lc = kx.shape[1]
    cur = lambda b, n: (b, n, 0)
    prev = lambda b, n: (b, jnp.maximum(n * ATT_QB - 1, 0), 0)
    nxt = lambda b, n: (b, jnp.minimum((n + 1) * ATT_QB, nb - 1), 0)
    ctx = lambda b, n: (b, 0, 0)
    edge_spec = lambda m: pl.BlockSpec((1, blk, D_KV), m)
    cur_spec = pl.BlockSpec((1, ATT_QB * blk, D_KV), cur)
    n_slots = ATT_QB * ATT_GROUP
    return pl.pallas_call(
        _attn_kernel,
        grid=(bsz, nb // ATT_QB),
        in_specs=[pl.BlockSpec((1, ATT_QB * blk, D_Q), cur),
                  edge_spec(prev), cur_spec, edge_spec(nxt),
                  edge_spec(prev), cur_spec, edge_spec(nxt),
                  pl.BlockSpec((1, lc, D_KV), ctx), pl.BlockSpec((1, lc, D_KV), ctx),
                  pl.BlockSpec(memory_space=pltpu.SMEM)],
        out_specs=pl.BlockSpec((1, ATT_QB * blk, D_Q), cur),
        out_shape=jax.ShapeDtypeStruct((bsz, n_rows, D_Q), BF16),
        scratch_shapes=[pltpu.VMEM((n_slots, ATT_KV_HEADS * blk, 3 * blk + lc), F32),
                        pltpu.VMEM((n_slots, ATT_KV_HEADS * blk, 3 * blk + lc), BF16),
                        pltpu.VMEM((2 * ATT_QB, blk, blk), F32)],
        compiler_params=pltpu.CompilerParams(dimension_semantics=("arbitrary", "arbitrary"),
                                             vmem_limit_bytes=_vmem_limit(48 << 20)),
        name="attn",
    )(q, k, k, k, v, v, v, kx, vx, sink)


def _merge_kernel(y_ref, a_ref, gate_ref, x_ref, mod_ref, wos_ref, woa_ref, wout_ref, o_ref):
    gates = gate_ref[0].astype(F32)
    m = (gates[:, :D_MODEL] * _dot(y_ref[0], wos_ref[...])
         + gates[:, D_MODEL:] * _dot(a_ref[0], woa_ref[...]))
    o = _dot(m.astype(BF16), wout_ref[...])
    gt1 = mod_ref[0, :, 2 * D_MODEL:3 * D_MODEL]
    o_ref[0] = x_ref[0] + gt1 * o


def _merge_call(yn, att, gates, x, modv, w_os, w_oa, w_out, tm):
    bsz, n_rows, _ = x.shape
    row_map = lambda b, i: (b, i, 0)
    return pl.pallas_call(
        _merge_kernel,
        grid=(bsz, n_rows // tm),
        in_specs=[pl.BlockSpec((1, tm, D_SSD), row_map), pl.BlockSpec((1, tm, D_Q), row_map),
                  pl.BlockSpec((1, tm, 2 * D_MODEL), row_map), pl.BlockSpec((1, tm, D_MODEL), row_map),
                  pl.BlockSpec((1, 1, N_MOD * D_MODEL), lambda b, i: (b, 0, 0)),
                  _resident((D_SSD, D_MODEL)), _resident((D_Q, D_MODEL)), _resident((D_MODEL, D_MODEL))],
        out_specs=pl.BlockSpec((1, tm, D_MODEL), row_map),
        out_shape=jax.ShapeDtypeStruct((bsz, n_rows, D_MODEL), F32),
        compiler_params=pltpu.CompilerParams(dimension_semantics=("arbitrary", "arbitrary"),
                                             vmem_limit_bytes=_vmem_limit(48 << 20)),
        name="merge",
    )(yn, att, gates, x, modv, w_os, w_oa, w_out)


FF_CHUNK = 256


def _ffn_kernel(x_ref, xp_ref, xn_ref, mod_ref, g_ref, wup_ref, cw_ref, cb_ref, wdn_ref, fg_ref,
                o_ref, ua_scr, ub_scr, act_scr, *, tm):
    i = pl.program_id(1)
    n_i = pl.num_programs(1)
    g = g_ref[...]
    shift = mod_ref[0, :, 3 * D_MODEL:4 * D_MODEL]
    scale = mod_ref[0, :, 4 * D_MODEL:5 * D_MODEL]
    gt2 = mod_ref[0, :, 5 * D_MODEL:6 * D_MODEL]
    x = x_ref[0]
    h = _modulated_norm(x, g, shift, scale)
    hp = _modulated_norm(xp_ref[0], g, shift, scale) * (i > 0).astype(F32)
    hn = _modulated_norm(xn_ref[0], g, shift, scale) * (i < n_i - 1).astype(F32)
    h_ext = jnp.concatenate([h, hn, hp], axis=0).astype(BF16)
    for c in range(D_FF // FF_CHUNK):
        c0 = c * FF_CHUNK
        ua_scr[c] = _dot(h_ext, wup_ref[:, c0:c0 + FF_CHUNK])
        ub_scr[c] = _dot(h_ext, wup_ref[:, D_FF + c0:D_FF + c0 + FF_CHUNK])
    for c in range(D_FF // FF_CHUNK):
        c0 = c * FF_CHUNK
        b0 = D_FF + c0
        a = _conv3_rows(ua_scr[c], tm, cw_ref[:, c0:c0 + FF_CHUNK], cb_ref[:, c0:c0 + FF_CHUNK])
        b = _conv3_rows(ub_scr[c], tm, cw_ref[:, b0:b0 + FF_CHUNK], cb_ref[:, b0:b0 + FF_CHUNK])
        act_scr[:, c0:c0 + FF_CHUNK] = (_silu(a) * b).astype(BF16)
    f = _dot(act_scr[...], wdn_ref[...])
    x2 = x + gt2 * f
    ms = jnp.mean(x2 * x2, axis=-1, keepdims=True)
    o_ref[0] = x2 * lax.rsqrt(ms + EPS) * fg_ref[...]


def _ffn_call(x1, modv, norm_g, w_up, conv_w, conv_b, w_down, final_g, tm):
    bsz, n_rows, _ = x1.shape
    prev_map, next_map = _halo_maps(tm, n_rows)
    row_map = lambda b, i: (b, i, 0)
    return pl.pallas_call(
        functools.partial(_ffn_kernel, tm=tm),
        grid=(bsz, n_rows // tm),
        in_specs=[pl.BlockSpec((1, tm, D_MODEL), row_map),
                  pl.BlockSpec((1, HALO, D_MODEL), prev_map),
                  pl.BlockSpec((1, HALO, D_MODEL), next_map),
                  pl.BlockSpec((1, 1, N_MOD * D_MODEL), lambda b, i: (b, 0, 0)),
                  _resident((1, D_MODEL)),
                  _resident((D_MODEL, 2 * D_FF)), _resident((3, 2 * D_FF)), _resident((1, 2 * D_FF)),
                  _resident((D_FF, D_MODEL)), _resident((1, D_MODEL))],
        out_specs=pl.BlockSpec((1, tm, D_MODEL), row_map),
        out_shape=jax.ShapeDtypeStruct((bsz, n_rows, D_MODEL), F32),
        scratch_shapes=[pltpu.VMEM((D_FF // FF_CHUNK, tm + 2 * HALO, FF_CHUNK), F32),
                        pltpu.VMEM((D_FF // FF_CHUNK, tm + 2 * HALO, FF_CHUNK), F32),
                        pltpu.VMEM((tm, D_FF), BF16)],
        compiler_params=pltpu.CompilerParams(dimension_semantics=("arbitrary", "arbitrary"),
                                             vmem_limit_bytes=_vmem_limit(48 << 20)),
        name="ffn",
    )(x1, x1, x1, modv, norm_g, w_up, conv_w, conv_b, w_down, final_g)


def _rope_tables(n_rows):
    half = ATT_HEADDIM // 2
    inv = ROPE_BASE ** (-jnp.arange(0, half, 2, dtype=F32) / half)
    pos = jnp.arange(n_rows)
    rowp = (pos // GRID_W).astype(F32)[:, None] * inv[None]
    colp = (pos % GRID_W).astype(F32)[:, None] * inv[None]
    cos = jnp.concatenate([jnp.cos(rowp), jnp.cos(rowp), jnp.cos(colp), jnp.cos(colp)], axis=-1)
    sin = jnp.concatenate([-jnp.sin(rowp), jnp.sin(rowp), -jnp.sin(colp), jnp.sin(colp)], axis=-1)
    reps = LANES // ATT_HEADDIM
    return jnp.tile(cos, (1, reps)), jnp.tile(sin, (1, reps))


def kernel(x, c, ctx, c_ctx, w_mod, b_mod, norm1_g, norm2_g, w_in, ssd_conv_w, ssd_conv_b, ssd_dt_bias,
           ssd_a_log, ssd_d, ssd_norm_g, w_o_ssd, w_o_att, att_sink, w_out, w_up, ffn_conv_w, ffn_conv_b,
           w_down, final_g):
    depth = w_mod.shape[0]
    assert depth == 1, "single-layer block"
    l = 0
    bsz, n_lat, _ = x.shape
    assert bsz + 1 <= SUBLANES

    cc = jnp.concatenate([c, c_ctx[None], jnp.zeros((SUBLANES - bsz - 1, D_MODEL), F32)], axis=0)
    mod_all = _mod_call(cc, w_mod[l], b_mod[l][None])
    mod_lat = mod_all[:bsz, None, :]
    mod_ctx = mod_all[bsz:bsz + 1, None, :]

    wi = w_in[l]
    w_q = (wi[:, Q0:K0].reshape(D_MODEL, ATT_KV_HEADS, ATT_GROUP, ATT_HEADDIM)
           .transpose(0, 2, 1, 3).reshape(D_MODEL, D_Q))
    w = jnp.concatenate([wi[:, XBC0:DT0], wi[:, DT0:Q0], wi[:, DT0:Q0], wi[:, K0:G0],
                         wi[:, Z0:XBC0], w_q, wi[:, G0:D_IN]], axis=1).astype(BF16)
    dtb = ssd_dt_bias[l].reshape(1, 2 * SSD_HEADS)
    a_mult = -jnp.exp(ssd_a_log[l].reshape(1, 2 * SSD_HEADS))
    consts = {"conv_w": ssd_conv_w[l], "conv_b": ssd_conv_b[l][None],
              "dt_bias": jnp.concatenate([dtb, dtb], axis=1),
              "a_mult": jnp.concatenate([jnp.ones_like(a_mult), a_mult], axis=1)}
    rope = _rope_tables(n_lat)
    g1 = norm1_g[l][None]

    xbc_c, dtla_c, k_c, v_c = _inproj_call(ctx, mod_ctx, False, g1, w, consts, None, ctx.shape[1], False)
    xbc_l, dtla_l, k_l, v_l, zg, q_l, gates = _inproj_call(x, mod_lat, True, g1, w, consts, rope, 256, True)

    (hf_c,) = _ssd_call(xbc_c, dtla_c, None, False, False)
    (hb_c,) = _ssd_call(xbc_c, dtla_c, None, True, False)
    yf, _ = _ssd_call(xbc_l, dtla_l, hf_c, False, True)
    dskip = jnp.repeat(ssd_d[l], SSD_HEADDIM)[None]
    yn, _ = _ssd_call(xbc_l, dtla_l, hb_c, True, True, fuse=(yf, zg, dskip, ssd_norm_g[l][None]))

    sink = att_sink[l].astype(F32)
    att = _attn_call(q_l, k_l, v_l, k_c, v_c, sink)

    w_oa = (w_o_att[l].reshape(ATT_KV_HEADS, ATT_GROUP, ATT_HEADDIM, D_MODEL)
            .transpose(1, 0, 2, 3).reshape(D_Q, D_MODEL))
    x1 = _merge_call(yn, att, gates, x, mod_lat, w_o_ssd[l].astype(BF16),
                     w_oa.astype(BF16), w_out[l].astype(BF16), 512)
    return _ffn_call(x1, mod_lat, norm2_g[l][None], w_up[l].astype(BF16), ffn_conv_w[l],
                     ffn_conv_b[l][None], w_down[l].astype(BF16), final_g[None], 512)
```

```python
import functools
import math

import jax
import jax.numpy as jnp
import numpy as np
from jax import lax
from jax.experimental import pallas as pl
from jax.experimental.pallas import tpu as pltpu

F32 = jnp.float32
BF16 = jnp.bfloat16

D_MODEL = 1024
GRID_W = 64
EPS = 1e-6
N_MOD = 6
D_SSD = 2048
SSD_HEADDIM = 64
SSD_HEADS = 32
SSD_GROUPS = 4
SSD_HPG = 8
SSD_STATE = 128
SSD_CHUNK = 128
D_BC = SSD_GROUPS * SSD_STATE
D_XBC = D_SSD + 2 * D_BC
ATT_HEADS = 16
ATT_KV_HEADS = 4
ATT_GROUP = 4
ATT_HEADDIM = 64
ATT_BLOCK = 128
D_Q = ATT_HEADS * ATT_HEADDIM
D_KV = ATT_KV_HEADS * ATT_HEADDIM
ROPE_BASE = 10000.0
D_FF = 2816
Z0 = 0
XBC0 = Z0 + D_SSD
DT0 = XBC0 + D_XBC
Q0 = DT0 + 2 * SSD_HEADS
K0 = Q0 + D_Q
V0 = K0 + D_KV
G0 = V0 + D_KV
D_IN = G0 + 2 * D_MODEL

V7X_VMEM_BYTES = 64 * 1024 * 1024
LANES = 128
SUBLANES = 8
HALO = SUBLANES
NEG_BIG = -1e30
LOG2E = 1.4426950408889634


def _vmem_limit(nbytes):
    return int(min(nbytes, V7X_VMEM_BYTES - 8 * 1024 * 1024))


def _resident(shape, index=None):
    index = (0,) * len(shape) if index is None else tuple(index)
    return pl.BlockSpec(shape, lambda *_: index, pipeline_mode=pl.Buffered(1))


def _sigmoid(x):
    return 1.0 / (1.0 + jnp.exp2(x * (-LOG2E)))


def _silu(x):
    return x * _sigmoid(x)


def _dot(a, b):
    return jnp.dot(a, b, preferred_element_type=F32)


def _dot_nt(a, b):
    return lax.dot_general(a, b, (((1,), (1,)), ((), ())), preferred_element_type=F32)


def _modulated_norm(x, g, shift, scale):
    ms = jnp.mean(x * x, axis=-1, keepdims=True)
    return x * lax.rsqrt(ms + EPS) * g * (1.0 + scale) + shift


def _mod_kernel(c_ref, w_ref, b_ref, o_ref):
    s = _silu(c_ref[...])
    o_ref[...] = _dot(s.astype(BF16), w_ref[...].astype(BF16)) + b_ref[...]


def _mod_call(cc, w_mod, b_mod):
    n = w_mod.shape[1]
    tn = 1536
    return pl.pallas_call(
        _mod_kernel,
        grid=(n // tn,),
        in_specs=[pl.BlockSpec((SUBLANES, D_MODEL), lambda j: (0, 0)),
                  pl.BlockSpec((D_MODEL, tn), lambda j: (0, j)),
                  pl.BlockSpec((1, tn), lambda j: (0, j))],
        out_specs=pl.BlockSpec((SUBLANES, tn), lambda j: (0, j)),
        out_shape=jax.ShapeDtypeStruct((SUBLANES, n), F32),
        compiler_params=pltpu.CompilerParams(dimension_semantics=("arbitrary",),
                                             vmem_limit_bytes=_vmem_limit(40 << 20)),
        name="mod",
    )(cc, w_mod, b_mod)


def _halo_maps(tm, n_rows):
    hb = tm // HALO
    last = n_rows // HALO - 1
    prev_map = lambda b, i: (b, jnp.maximum(i * hb - 1, 0), 0)
    next_map = lambda b, i: (b, jnp.minimum((i + 1) * hb, last), 0)
    return prev_map, next_map


def _conv3_rows(u, tm, w, b):
    n = u.shape[0]
    before = pltpu.roll(u, 1, 0)[0:tm]
    after = pltpu.roll(u, n - 1, 0)[0:tm]
    return before * w[0:1] + u[0:tm] * w[1:2] + after * w[2:3] + b


def _store_normed_rows(h_scr, x_ref, xp_ref, xn_ref, g, shift, scale, tm):
    i = pl.program_id(1)
    n_i = pl.num_programs(1)
    h = _modulated_norm(x_ref[0], g, shift, scale)
    hp = _modulated_norm(xp_ref[0], g, shift, scale) * (i > 0).astype(F32)
    hn = _modulated_norm(xn_ref[0], g, shift, scale) * (i < n_i - 1).astype(F32)
    h_scr[0:tm, :] = h.astype(BF16)
    h_scr[tm:tm + 2 * HALO, :] = jnp.concatenate([hn, hp], axis=0).astype(BF16)


def _rope(t, cos, sin_signed, n_tiles):
    width = t.shape[1]
    lane = lax.broadcasted_iota(jnp.int32, t.shape, 1)
    first = (lane % 32) < 16
    partner = jnp.where(first, pltpu.roll(t, width - 16, 1), pltpu.roll(t, 16, 1))
    cos_t = jnp.concatenate([cos] * n_tiles, axis=1)
    sin_t = jnp.concatenate([sin_signed] * n_tiles, axis=1)
    return t * cos_t + partner * sin_t


XBC_WINDOWS = (512, 2560)
assert sum(XBC_WINDOWS) == D_XBC
QKVG_K0 = D_Q
QKVG_V0 = QKVG_K0 + D_KV
QKVG_G0 = QKVG_V0 + D_KV
QKVG_END = QKVG_G0 + 2 * D_MODEL
P_DT0 = 0
P_B0 = LANES
P_Z0 = P_B0 + QKVG_END


def _inproj_kernel(*refs, tm, latent):
    n_win = len(XBC_WINDOWS)
    x_ref, xp_ref, xn_ref, mod_ref, g_ref = refs[:5]
    wx_refs = refs[5:5 + n_win]
    rest = refs[5 + n_win:]
    if latent:
        (wdt_ref, wb_ref, wz_ref, cw_ref, cb_ref, dtb_ref, am_ref, cos_ref, sin_ref,
         xbc_o, dtla_o, k_o, v_o, z_o, q_o, gate_o, u_scr, p_scr, h_scr) = rest
        pk0, pv0 = P_B0 + QKVG_K0, P_B0 + QKVG_V0
    else:
        (wdt_ref, wb_ref, cw_ref, cb_ref, dtb_ref, am_ref,
         xbc_o, dtla_o, k_o, v_o, u_scr, p_scr, h_scr) = rest
        pk0, pv0 = P_B0, P_B0 + D_KV
    i = pl.program_id(1)

    @pl.when(i >= 0)
    def _():
        _store_normed_rows(h_scr, x_ref, xp_ref, xn_ref, g_ref[...], mod_ref[0, :, 0:D_MODEL],
                           mod_ref[0, :, D_MODEL:2 * D_MODEL], tm)

    c0 = 0
    for wx_ref in wx_refs:
        u_scr[:, c0:c0 + wx_ref.shape[1]] = _dot(h_scr[...], wx_ref[...])
        c0 += wx_ref.shape[1]
    p_scr[:, P_DT0:P_B0] = _dot(h_scr[0:tm, :], wdt_ref[...])
    p_scr[:, P_B0:P_B0 + wb_ref.shape[1]] = _dot(h_scr[0:tm, :], wb_ref[...])
    if latent:
        p_scr[:, P_Z0:P_Z0 + D_SSD] = _dot(h_scr[0:tm, :], wz_ref[...])

    cchunk = 512
    for c0 in range(0, D_XBC, cchunk):
        cols = slice(c0, c0 + cchunk)
        n = tm + 2 * HALO
        before = jnp.concatenate([u_scr[n - 1:n, cols], u_scr[0:tm - 1, cols]], axis=0)
        after = jnp.concatenate([u_scr[1:tm, cols], u_scr[tm:tm + 1, cols]], axis=0)
        y = (before * cw_ref[0:1, cols] + u_scr[0:tm, cols] * cw_ref[1:2, cols]
             + after * cw_ref[2:3, cols] + cb_ref[:, cols])
        xbc_o[0, :, cols] = _silu(y).astype(BF16)

    raw = p_scr[:, P_DT0:P_B0]
    lane = lax.broadcasted_iota(jnp.int32, raw.shape, 1)
    raw = jnp.where(lane < 2 * SSD_HEADS, raw, pltpu.roll(raw, 2 * SSD_HEADS, 1)) + dtb_ref[...]
    dt = jnp.maximum(raw, 0.0) + jnp.log1p(jnp.exp(-jnp.abs(raw)))
    dtla_o[0] = jnp.where(lane < 2 * SSD_HEADS, dt, dt * am_ref[...])

    kf = p_scr[:, pk0:pk0 + D_KV]
    if latent:
        cos = cos_ref[...]
        sin = sin_ref[...]
        kf = _rope(kf, cos, sin, D_KV // LANES)
    k_o[0] = kf.astype(BF16)
    v_o[0] = p_scr[:, pv0:pv0 + D_KV].astype(BF16)
    if latent:
        z_o[0] = _silu(p_scr[:, P_Z0:P_Z0 + D_SSD]).astype(BF16)
        qf = _rope(p_scr[:, P_B0:P_B0 + D_Q], cos, sin, D_Q // LANES)
        q_o[0] = (qf * (ATT_HEADDIM ** -0.5 * LOG2E)).astype(BF16)
        gate_o[0] = _sigmoid(p_scr[:, P_B0 + QKVG_G0:P_B0 + QKVG_END]).astype(BF16)


def _inproj_call(x, modv, per_batch_mod, norm_g, w_in_bf16, w_qkvg, consts, rope, tm, latent):
    bsz, n_rows, _ = x.shape
    prev_map, next_map = _halo_maps(tm, n_rows)
    row_map = lambda b, i: (b, i, 0)
    mod_map = (lambda b, i: (b, 0, 0)) if per_batch_mod else (lambda b, i: (0, 0, 0))
    assert DT0 % LANES == 0
    xbc_specs, start = [], XBC0
    for width in XBC_WINDOWS:
        assert start % width == 0
        xbc_specs.append(_resident((D_MODEL, width), (0, start // width)))
        start += width
    n_win = len(XBC_WINDOWS)
    in_specs = ([pl.BlockSpec((1, tm, D_MODEL), row_map),
                 pl.BlockSpec((1, HALO, D_MODEL), prev_map),
                 pl.BlockSpec((1, HALO, D_MODEL), next_map),
                 pl.BlockSpec((1, 1, N_MOD * D_MODEL), mod_map),
                 _resident((1, D_MODEL))]
                + xbc_specs
                + [_resident((D_MODEL, LANES), (0, DT0 // LANES))])
    args = [x, x, x, modv, norm_g] + [w_in_bf16] * (n_win + 1)
    if latent:
        in_specs += [_resident((D_MODEL, QKVG_END)), _resident((D_MODEL, D_SSD), (0, 0))]
        args += [w_qkvg, w_in_bf16]
    else:
        assert QKVG_K0 % (2 * D_KV) == 0
        in_specs += [_resident((D_MODEL, 2 * D_KV), (0, QKVG_K0 // (2 * D_KV)))]
        args += [w_qkvg]
    in_specs += [_resident((3, D_XBC)), _resident((1, D_XBC)), _resident((1, LANES)), _resident((1, LANES))]
    args += [consts["conv_w"], consts["conv_b"], consts["dt_bias"], consts["a_mult"]]
    out_shape = [jax.ShapeDtypeStruct((bsz, n_rows, D_XBC), BF16),
                 jax.ShapeDtypeStruct((bsz, n_rows, LANES), F32),
                 jax.ShapeDtypeStruct((bsz, n_rows, D_KV), BF16),
                 jax.ShapeDtypeStruct((bsz, n_rows, D_KV), BF16)]
    out_specs = [pl.BlockSpec((1, tm, D_XBC), row_map), pl.BlockSpec((1, tm, LANES), row_map),
                 pl.BlockSpec((1, tm, D_KV), row_map), pl.BlockSpec((1, tm, D_KV), row_map)]
    if latent:
        in_specs += [pl.BlockSpec((tm, LANES), lambda b, i: (i, 0)),
                     pl.BlockSpec((tm, LANES), lambda b, i: (i, 0))]
        args += [rope[0], rope[1]]
        out_shape += [jax.ShapeDtypeStruct((bsz, n_rows, D_SSD), BF16),
                      jax.ShapeDtypeStruct((bsz, n_rows, D_Q), BF16),
                      jax.ShapeDtypeStruct((bsz, n_rows, 2 * D_MODEL), BF16)]
        out_specs += [pl.BlockSpec((1, tm, D_SSD), row_map), pl.BlockSpec((1, tm, D_Q), row_map),
                      pl.BlockSpec((1, tm, 2 * D_MODEL), row_map)]
    return pl.pallas_call(
        functools.partial(_inproj_kernel, tm=tm, latent=latent),
        grid=(bsz, n_rows // tm),
        in_specs=in_specs, out_specs=out_specs, out_shape=out_shape,
        scratch_shapes=[pltpu.VMEM((tm + 2 * HALO, D_XBC), F32),
                        pltpu.VMEM((tm, (P_Z0 + D_SSD) if latent else (P_B0 + 2 * D_KV)), F32),
                        pltpu.VMEM((tm + 2 * HALO, D_MODEL), BF16)],
        compiler_params=pltpu.CompilerParams(dimension_semantics=("arbitrary", "arbitrary"),
                                             vmem_limit_bytes=_vmem_limit(56 << 20)),
        name="inproj_lat" if latent else "inproj_ctx",
    )(*args)


QUAD = 4
QW = QUAD * SSD_HEADDIM


def _ssd_kernel(*refs, reverse, has_h0, with_y, fuse_out, cps):
    T = SSD_CHUNK
    it = iter(refs)
    xbc_ref = next(it)
    dtla_ref = next(it)
    h0_ref = next(it) if has_h0 else None
    if fuse_out:
        yf_ref, z_ref, dskip_ref, ng_ref = next(it), next(it), next(it), next(it)
    y_o = next(it) if with_y else None
    hfin_o = next(it)
    s_scr = next(it)
    rhs_scr = next(it)
    lhs_scr = next(it)
    y_scr = next(it) if with_y else None

    i = pl.program_id(1)
    n_i = pl.num_programs(1)

    @pl.when(i == 0)
    def _():
        if has_h0:
            s_scr[...] = h0_ref[0]
        else:
            s_scr[...] = jnp.zeros_like(s_scr)
        rhs_scr[...] = jnp.zeros_like(rhs_scr)

    d = 1 if reverse else 0
    dt_c0 = d * SSD_HEADS
    la_c0 = 2 * SSD_HEADS + d * SSD_HEADS
    lane = lax.broadcasted_iota(jnp.int32, (T, LANES), 1)
    row = lax.broadcasted_iota(jnp.int32, (T, T), 0)
    col = lax.broadcasted_iota(jnp.int32, (T, T), 1)
    keep = (col >= row) if reverse else (col <= row)
    tri = keep.astype(F32)
    end_row = 0 if reverse else T - 1
    lane_pair = lax.broadcasted_iota(jnp.int32, (1, LANES), 1) < SSD_HEADDIM
    n_quads = SSD_HEADS // QUAD

    def chunk_constants(ck):
        rows = pl.ds(ck * T, T)
        dtla = dtla_ref[0, rows, :]
        dtla2 = jnp.where(lane >= 2 * SSD_HEADS, dtla * LOG2E, dtla)
        cs = jnp.dot(tri, dtla2, preferred_element_type=F32, precision=lax.Precision.HIGHEST)
        cs_t = cs.T
        dtla_t = dtla.T
        a_t = cs_t[la_c0:la_c0 + SSD_HEADS]
        a_end_t = cs_t[la_c0:la_c0 + SSD_HEADS, end_row:end_row + 1]
        dt_t = dtla_t[dt_c0:dt_c0 + SSD_HEADS]
        w_t = dt_t * jnp.exp2(a_end_t - a_t)
        adt_t = a_t - jnp.log2(dt_t)
        groups = []
        for g in range(SSD_GROUPS):
            b_g = xbc_ref[0, rows, D_SSD + g * SSD_STATE:D_SSD + (g + 1) * SSD_STATE]
            bt_g = b_g.astype(F32).T
            c_g = cb_g = None
            if with_y:
                c_g = xbc_ref[0, rows, D_SSD + D_BC + g * SSD_STATE:D_SSD + D_BC + (g + 1) * SSD_STATE]
                cb_g = _dot_nt(c_g, b_g)
            groups.append((bt_g, c_g, cb_g))
        return cs, w_t, adt_t, groups

    def scan_chunk(ck, constants):
        rows = pl.ds(ck * T, T)
        cs, w_t, adt_t, group_consts = constants

        def y_put(pc0, val):
            y_scr[ck, :, pc0:pc0 + LANES] = val

        def y_get(pc0):
            return y_scr[ck, :, pc0:pc0 + LANES]

        boff = T if with_y else 0
        groups = []
        for g, (bt_g, c_g, cb_g) in enumerate(group_consts):
            yoff_g = None
            if with_y:
                gc0 = g * SSD_HPG * SSD_HEADDIM
                s_g = s_scr[:, gc0:gc0 + SSD_HPG * SSD_HEADDIM].astype(BF16)
                yoff_g = _dot(c_g, s_g)
            groups.append((bt_g, cb_g, yoff_g))

        def prep(q):
            g = q // (SSD_HPG // QUAD)
            bt_g, cb_g, yoff_g = groups[g]
            gc0 = g * SSD_HPG * SSD_HEADDIM
            h0 = q * QUAD
            c0 = h0 * SSD_HEADDIM
            slot = ck * n_quads + q
            for j in range(QUAD):
                rhs_scr[slot, pl.ds(j * T, T), j * SSD_HEADDIM:(j + 1) * SSD_HEADDIM] = (
                    xbc_ref[0, rows, c0 + j * SSD_HEADDIM:c0 + (j + 1) * SSD_HEADDIM])
            decays = []
            ar_even = None
            for j in range(QUAD):
                hh = h0 + j
                lhs_scr[slot, boff:boff + SSD_STATE, j * T:(j + 1) * T] = (
                    bt_g * w_t[hh:hh + 1]).astype(BF16)
                ar = jnp.broadcast_to(cs[:, la_c0 + hh:la_c0 + hh + 1], (T, LANES))
                if with_y:
                    seg = jnp.where(keep, ar - adt_t[hh:hh + 1], NEG_BIG)
                    lhs_scr[slot, 0:T, j * T:(j + 1) * T] = (cb_g * jnp.exp2(seg)).astype(BF16)
                if j % 2 == 0:
                    ar_even = ar
                    continue
                ar_pair = jnp.where(lane_pair, ar_even, ar)
                pc0 = c0 + (j // 2) * LANES
                if with_y:
                    yo = yoff_g[:, pc0 - gc0:pc0 - gc0 + LANES]
                    y_put(pc0, yo * jnp.exp2(ar_pair))
                decays.append(jnp.exp2(ar_pair[end_row:end_row + 1]))
            return decays

        def finish(q, decays):
            slot = ck * n_quads + q
            c0 = q * QUAD * SSD_HEADDIM
            res = _dot(lhs_scr[slot], rhs_scr[slot])
            for p in range(QUAD // 2):
                pc0 = c0 + p * LANES
                if with_y:
                    y_pair = y_get(pc0) + res[:T, p * LANES:(p + 1) * LANES]
                    if fuse_out:
                        y_put(pc0, y_pair)
                    else:
                        y_o[0, rows, pc0:pc0 + LANES] = y_pair.astype(y_o.dtype)
                s_scr[:, pc0:pc0 + LANES] = (s_scr[:, pc0:pc0 + LANES] * decays[p]
                                             + res[boff:, p * LANES:(p + 1) * LANES])

        for q in range(n_quads):
            finish(q, prep(q))

        if fuse_out:
            xs = xbc_ref[0, rows, 0:D_SSD].astype(F32)
            y = y_scr[ck] + yf_ref[0, rows, :].astype(F32) + dskip_ref[...] * xs
            gy = y * z_ref[0, rows, :].astype(F32)
            ms = jnp.mean(gy * gy, axis=-1, keepdims=True)
            y_o[0, rows, :] = (gy * lax.rsqrt(ms + EPS) * ng_ref[...]).astype(y_o.dtype)

    order = list(range(cps - 1, -1, -1) if reverse else range(cps))
    constants = {ck: chunk_constants(ck) for ck in order}
    for ck in order:
        scan_chunk(ck, constants[ck])

    @pl.when(i == n_i - 1)
    def _():
        hfin_o[0] = s_scr[...]


def _ssd_call(xbc, dtla, h0, reverse, with_y, fuse=None, cps=4):
    bsz, n_rows, _ = xbc.shape
    T = SSD_CHUNK
    cps = min(cps, n_rows // T)
    rows = cps * T
    ns = n_rows // rows
    cmap = (lambda b, i: (b, ns - 1 - i, 0)) if reverse else (lambda b, i: (b, i, 0))
    bmap = lambda b, i: (b, 0, 0)
    in_specs = [pl.BlockSpec((1, rows, D_XBC), cmap), pl.BlockSpec((1, rows, LANES), cmap)]
    args = [xbc, dtla]
    if h0 is not None:
        in_specs.append(pl.BlockSpec((1, SSD_STATE, D_SSD), bmap))
        args.append(h0)
    if fuse is not None:
        yf, zg, dskip, ng = fuse
        in_specs += [pl.BlockSpec((1, rows, D_SSD), cmap), pl.BlockSpec((1, rows, D_SSD), cmap),
                     _resident((1, D_SSD)), _resident((1, D_SSD))]
        args += [yf, zg, dskip, ng]
    out_shape, out_specs = [], []
    if with_y:
        out_shape.append(jax.ShapeDtypeStruct((bsz, n_rows, D_SSD), BF16))
        out_specs.append(pl.BlockSpec((1, rows, D_SSD), cmap))
    out_shape.append(jax.ShapeDtypeStruct((bsz, SSD_STATE, D_SSD), F32))
    out_specs.append(pl.BlockSpec((1, SSD_STATE, D_SSD), bmap))
    scratch = [pltpu.VMEM((SSD_STATE, D_SSD), F32),
               pltpu.VMEM((cps * SSD_HEADS // QUAD, QUAD * T, QW), BF16),
               pltpu.VMEM((cps * SSD_HEADS // QUAD, (T if with_y else 0) + SSD_STATE, QUAD * T), BF16)]
    if with_y:
        scratch.append(pltpu.VMEM((cps, T, D_SSD), F32))
    name = "ssd_" + ("rev" if reverse else "fwd") + ("_y" if with_y else "_state") + ("_out" if fuse else "")
    return pl.pallas_call(
        functools.partial(_ssd_kernel, reverse=reverse, has_h0=h0 is not None, with_y=with_y,
                          fuse_out=fuse is not None, cps=cps),
        grid=(bsz, ns),
        in_specs=in_specs, out_specs=out_specs, out_shape=out_shape,
        scratch_shapes=scratch,
        compiler_params=pltpu.CompilerParams(dimension_semantics=("arbitrary", "arbitrary"),
                                             vmem_limit_bytes=_vmem_limit(52 << 20)),
        name=name,
    )(*args)


ATT_STRIP = 16
ATT_QB = 2


def _attn_kernel(q_ref, kp_ref, kc_ref, kn_ref, vp_ref, vc_ref, vn_ref, kx_ref, vx_ref, sink_ref, o_ref,
                 s_scr, p_scr, bias_scr):
    blk = ATT_BLOCK
    n = pl.program_id(1)
    n_n = pl.num_programs(1)
    row = lax.broadcasted_iota(jnp.int32, (blk, blk), 0)
    col = lax.broadcasted_iota(jnp.int32, (blk, blk), 1)
    lane_kv = lax.broadcasted_iota(jnp.int32, (1, D_KV), 1) // ATT_HEADDIM
    kv_masks = [(lane_kv == j) for j in range(ATT_KV_HEADS)]
    kwin = [kp_ref[0]] + [kc_ref[0, sb * blk:(sb + 1) * blk, :] for sb in range(ATT_QB)] + [kn_ref[0]]
    vwin = [vp_ref[0]] + [vc_ref[0, sb * blk:(sb + 1) * blk, :] for sb in range(ATT_QB)] + [vn_ref[0]]
    vcats = []
    for sb in range(ATT_QB):
        prev_ok = (col >= row) if sb > 0 else jnp.logical_and(col >= row, n > 0)
        next_ok = (col <= row) if sb < ATT_QB - 1 else jnp.logical_and(col <= row, n < n_n - 1)
        bias_scr[2 * sb] = jnp.where(prev_ok, 0.0, NEG_BIG)
        bias_scr[2 * sb + 1] = jnp.where(next_ok, 0.0, NEG_BIG)
        kcat = jnp.concatenate(kwin[sb:sb + 3] + [kx_ref[0]], axis=0)
        vcats.append(jnp.concatenate(vwin[sb:sb + 3] + [vx_ref[0]], axis=0))
        nk = kcat.shape[0]
        for g in range(ATT_GROUP):
            qg = q_ref[0, sb * blk:(sb + 1) * blk, g * D_KV:(g + 1) * D_KV]
            qs = jnp.concatenate([jnp.where(kv_masks[j], qg, jnp.zeros_like(qg))
                                  for j in range(ATT_KV_HEADS)], axis=0)
            s_scr[sb * ATT_GROUP + g] = _dot_nt(qs, kcat)
    for sb, g in [(sb, g) for sb in range(ATT_QB) for g in range(ATT_GROUP)]:
        slot = sb * ATT_GROUP + g
        vcat = vcats[sb]
        for st in range(ATT_KV_HEADS * blk // ATT_STRIP):
            r0 = st * ATT_STRIP
            i0 = r0 % blk
            sink = sink_ref[(r0 // blk) * ATT_GROUP + g] * LOG2E
            rows = pl.ds(r0, ATT_STRIP)
            parts = [s_scr[slot, rows, 0:blk] + bias_scr[2 * sb, pl.ds(i0, ATT_STRIP), :],
                     s_scr[slot, rows, blk:2 * blk],
                     s_scr[slot, rows, 2 * blk:3 * blk] + bias_scr[2 * sb + 1, pl.ds(i0, ATT_STRIP), :]]
            parts += [s_scr[slot, rows, c0:c0 + LANES] for c0 in range(3 * blk, nk, LANES)]
            mel = parts[0]
            for t in parts[1:]:
                mel = jnp.maximum(mel, t)
            m = jnp.maximum(jnp.max(mel, axis=-1, keepdims=True), sink)
            ps = [jnp.exp2(t - m) for t in parts]
            tot = ps[0]
            for t in ps[1:]:
                tot = tot + t
            den = jnp.sum(tot, axis=-1, keepdims=True) + jnp.exp2(sink - m)
            inv = 1.0 / den
            for ci, t in enumerate(ps):
                p_scr[slot, rows, ci * LANES:(ci + 1) * LANES] = (t * inv).astype(BF16)
        o = _dot(p_scr[slot], vcat)
        og = jnp.zeros((blk, D_KV), F32)
        for j in range(ATT_KV_HEADS):
            og = jnp.where(kv_masks[j], o[j * blk:(j + 1) * blk], og)
        o_ref[0, sb * blk:(sb + 1) * blk, g * D_KV:(g + 1) * D_KV] = og.astype(o_ref.dtype)


def _attn_call(q, k, v, kx, vx, sink):
    bsz, n_rows, _ = q.shape
    blk = ATT_BLOCK
    nb = n_rows // blk
    lc = kx.shape[1]
    cur = lambda b, n: (b, n, 0)
    prev = lambda b, n: (b, jnp.maximum(n * ATT_QB - 1, 0), 0)
    nxt = lambda b, n: (b, jnp.minimum((n + 1) * ATT_QB, nb - 1), 0)
    ctx = lambda b, n: (b, 0, 0)
    edge_spec = lambda m: pl.BlockSpec((1, blk, D_KV), m)
    cur_spec = pl.BlockSpec((1, ATT_QB * blk, D_KV), cur)
    n_slots = ATT_QB * ATT_GROUP
    return pl.pallas_call(
        _attn_kernel,
        grid=(bsz, nb // ATT_QB),
        in_specs=[pl.BlockSpec((1, ATT_QB * blk, D_Q), cur),
                  edge_spec(prev), cur_spec, edge_spec(nxt),
                  edge_spec(prev), cur_spec, edge_spec(nxt),
                  pl.BlockSpec((1, lc, D_KV), ctx), pl.BlockSpec((1, lc, D_KV), ctx),
                  pl.BlockSpec(memory_space=pltpu.SMEM)],
        out_specs=pl.BlockSpec((1, ATT_QB * blk, D_Q), cur),
        out_shape=jax.ShapeDtypeStruct((bsz, n_rows, D_Q), BF16),
        scratch_shapes=[pltpu.VMEM((n_slots, ATT_KV_HEADS * blk, 3 * blk + lc), F32),
                        pltpu.VMEM((n_slots, ATT_KV_HEADS * blk, 3 * blk + lc), BF16),
                        pltpu.VMEM((2 * ATT_QB, blk, blk), F32)],
        compiler_params=pltpu.CompilerParams(dimension_semantics=("arbitrary", "arbitrary"),
                                             vmem_limit_bytes=_vmem_limit(48 << 20)),
        name="attn",
    )(q, k, k, k, v, v, v, kx, vx, sink)


def _merge_kernel(y_ref, a_ref, gate_ref, x_ref, mod_ref, wos_ref, woa_ref, wout_ref, o_ref):
    gates = gate_ref[0].astype(F32)
    m = (gates[:, :D_MODEL] * _dot(y_ref[0], wos_ref[...])
         + gates[:, D_MODEL:] * _dot(a_ref[0], woa_ref[...]))
    o = _dot(m.astype(BF16), wout_ref[...])
    gt1 = mod_ref[0, :, 2 * D_MODEL:3 * D_MODEL]
    o_ref[0] = x_ref[0] + gt1 * o


def _merge_call(yn, att, gates, x, modv, w_os, w_oa, w_out, tm):
    bsz, n_rows, _ = x.shape
    row_map = lambda b, i: (b, i, 0)
    return pl.pallas_call(
        _merge_kernel,
        grid=(bsz, n_rows // tm),
        in_specs=[pl.BlockSpec((1, tm, D_SSD), row_map), pl.BlockSpec((1, tm, D_Q), row_map),
                  pl.BlockSpec((1, tm, 2 * D_MODEL), row_map), pl.BlockSpec((1, tm, D_MODEL), row_map),
                  pl.BlockSpec((1, 1, N_MOD * D_MODEL), lambda b, i: (b, 0, 0)),
                  _resident((D_SSD, D_MODEL)), _resident((D_Q, D_MODEL)), _resident((D_MODEL, D_MODEL))],
        out_specs=pl.BlockSpec((1, tm, D_MODEL), row_map),
        out_shape=jax.ShapeDtypeStruct((bsz, n_rows, D_MODEL), F32),
        compiler_params=pltpu.CompilerParams(dimension_semantics=("arbitrary", "arbitrary"),
                                             vmem_limit_bytes=_vmem_limit(48 << 20)),
        name="merge",
    )(yn, att, gates, x, modv, w_os, w_oa, w_out)


FF_CHUNK = 256


def _ffn_kernel(x_ref, xp_ref, xn_ref, mod_ref, g_ref, wup_ref, cw_ref, cb_ref, wdn_ref, fg_ref,
                o_ref, ua_scr, ub_scr, act_scr, *, tm):
    i = pl.program_id(1)
    n_i = pl.num_programs(1)
    g = g_ref[...]
    shift = mod_ref[0, :, 3 * D_MODEL:4 * D_MODEL]
    scale = mod_ref[0, :, 4 * D_MODEL:5 * D_MODEL]
    gt2 = mod_ref[0, :, 5 * D_MODEL:6 * D_MODEL]
    x = x_ref[0]
    h = _modulated_norm(x, g, shift, scale)
    hp = _modulated_norm(xp_ref[0], g, shift, scale) * (i > 0).astype(F32)
    hn = _modulated_norm(xn_ref[0], g, shift, scale) * (i < n_i - 1).astype(F32)
    h_ext = jnp.concatenate([h, hn, hp], axis=0).astype(BF16)
    for c in range(D_FF // FF_CHUNK):
        c0 = c * FF_CHUNK
        ua_scr[c] = _dot(h_ext, wup_ref[:, c0:c0 + FF_CHUNK])
        ub_scr[c] = _dot(h_ext, wup_ref[:, D_FF + c0:D_FF + c0 + FF_CHUNK])
    for c in range(D_FF // FF_CHUNK):
        c0 = c * FF_CHUNK
        b0 = D_FF + c0
        a = _conv3_rows(ua_scr[c], tm, cw_ref[:, c0:c0 + FF_CHUNK], cb_ref[:, c0:c0 + FF_CHUNK])
        b = _conv3_rows(ub_scr[c], tm, cw_ref[:, b0:b0 + FF_CHUNK], cb_ref[:, b0:b0 + FF_CHUNK])
        act_scr[:, c0:c0 + FF_CHUNK] = (_silu(a) * b).astype(BF16)
    f = _dot(act_scr[...], wdn_ref[...])
    x2 = x + gt2 * f
    ms = jnp.mean(x2 * x2, axis=-1, keepdims=True)
    o_ref[0] = x2 * lax.rsqrt(ms + EPS) * fg_ref[...]


def _ffn_call(x1, modv, norm_g, w_up, conv_w, conv_b, w_down, final_g, tm):
    bsz, n_rows, _ = x1.shape
    prev_map, next_map = _halo_maps(tm, n_rows)
    row_map = lambda b, i: (b, i, 0)
    return pl.pallas_call(
        functools.partial(_ffn_kernel, tm=tm),
        grid=(bsz, n_rows // tm),
        in_specs=[pl.BlockSpec((1, tm, D_MODEL), row_map),
                  pl.BlockSpec((1, HALO, D_MODEL), prev_map),
                  pl.BlockSpec((1, HALO, D_MODEL), next_map),
                  pl.BlockSpec((1, 1, N_MOD * D_MODEL), lambda b, i: (b, 0, 0)),
                  _resident((1, D_MODEL)),
                  _resident((D_MODEL, 2 * D_FF)), _resident((3, 2 * D_FF)), _resident((1, 2 * D_FF)),
                  _resident((D_FF, D_MODEL)), _resident((1, D_MODEL))],
        out_specs=pl.BlockSpec((1, tm, D_MODEL), row_map),
        out_shape=jax.ShapeDtypeStruct((bsz, n_rows, D_MODEL), F32),
        scratch_shapes=[pltpu.VMEM((D_FF // FF_CHUNK, tm + 2 * HALO, FF_CHUNK), F32),
                        pltpu.VMEM((D_FF // FF_CHUNK, tm + 2 * HALO, FF_CHUNK), F32),
                        pltpu.VMEM((tm, D_FF), BF16)],
        compiler_params=pltpu.CompilerParams(dimension_semantics=("arbitrary", "arbitrary"),
                                             vmem_limit_bytes=_vmem_limit(48 << 20)),
        name="ffn",
    )(x1, x1, x1, modv, norm_g, w_up, conv_w, conv_b, w_down, final_g)


def _rope_tables(n_rows):
    half = ATT_HEADDIM // 2
    inv = ROPE_BASE ** (-jnp.arange(0, half, 2, dtype=F32) / half)
    pos = jnp.arange(n_rows)
    rowp = (pos // GRID_W).astype(F32)[:, None] * inv[None]
    colp = (pos % GRID_W).astype(F32)[:, None] * inv[None]
    cos = jnp.concatenate([jnp.cos(rowp), jnp.cos(rowp), jnp.cos(colp), jnp.cos(colp)], axis=-1)
    sin = jnp.concatenate([-jnp.sin(rowp), jnp.sin(rowp), -jnp.sin(colp), jnp.sin(colp)], axis=-1)
    reps = LANES // ATT_HEADDIM
    return jnp.tile(cos, (1, reps)), jnp.tile(sin, (1, reps))


def kernel(x, c, ctx, c_ctx, w_mod, b_mod, norm1_g, norm2_g, w_in, ssd_conv_w, ssd_conv_b, ssd_dt_bias,
           ssd_a_log, ssd_d, ssd_norm_g, w_o_ssd, w_o_att, att_sink, w_out, w_up, ffn_conv_w, ffn_conv_b,
           w_down, final_g):
    depth = w_mod.shape[0]
    assert depth == 1, "single-layer block"
    l = 0
    bsz, n_lat, _ = x.shape
    assert bsz + 1 <= SUBLANES

    cc = jnp.concatenate([c, c_ctx[None], jnp.zeros((SUBLANES - bsz - 1, D_MODEL), F32)], axis=0)
    mod_all = _mod_call(cc, w_mod[l], b_mod[l][None])
    mod_lat = mod_all[:bsz, None, :]
    mod_ctx = mod_all[bsz:bsz + 1, None, :]

    wi = w_in[l].astype(BF16)
    w_q = (wi[:, Q0:K0].reshape(D_MODEL, ATT_KV_HEADS, ATT_GROUP, ATT_HEADDIM)
           .transpose(0, 2, 1, 3).reshape(D_MODEL, D_Q))
    w_qkvg = jnp.concatenate([w_q, wi[:, K0:D_IN]], axis=1)
    dtb = ssd_dt_bias[l].reshape(1, 2 * SSD_HEADS)
    a_mult = -jnp.exp(ssd_a_log[l].reshape(1, 2 * SSD_HEADS))
    consts = {"conv_w": ssd_conv_w[l], "conv_b": ssd_conv_b[l][None],
              "dt_bias": jnp.concatenate([dtb, dtb], axis=1),
              "a_mult": jnp.concatenate([jnp.ones_like(a_mult), a_mult], axis=1)}
    rope = _rope_tables(n_lat)
    g1 = norm1_g[l][None]

    xbc_c, dtla_c, k_c, v_c = _inproj_call(ctx, mod_ctx, False, g1, wi, w_qkvg, consts, None,
                                           ctx.shape[1], False)
    xbc_l, dtla_l, k_l, v_l, zg, q_l, gates = _inproj_call(x, mod_lat, True, g1, wi, w_qkvg, consts, rope,
                                                           256, True)

    (hf_c,) = _ssd_call(xbc_c, dtla_c, None, False, False)
    (hb_c,) = _ssd_call(xbc_c, dtla_c, None, True, False)
    yf, _ = _ssd_call(xbc_l, dtla_l, hf_c, False, True)
    dskip = jnp.repeat(ssd_d[l], SSD_HEADDIM)[None]
    yn, _ = _ssd_call(xbc_l, dtla_l, hb_c, True, True, fuse=(yf, zg, dskip, ssd_norm_g[l][None]))

    sink = att_sink[l].astype(F32)
    att = _attn_call(q_l, k_l, v_l, k_c, v_c, sink)

    w_oa = (w_o_att[l].reshape(ATT_KV_HEADS, ATT_GROUP, ATT_HEADDIM, D_MODEL)
            .transpose(1, 0, 2, 3).reshape(D_Q, D_MODEL))
    x1 = _merge_call(yn, att, gates, x, mod_lat, w_o_ssd[l].astype(BF16),
                     w_oa.astype(BF16), w_out[l].astype(BF16), 512)
    return _ffn_call(x1, mod_lat, norm2_g[l][None], w_up[l].astype(BF16), ffn_conv_w[l],
                     ffn_conv_b[l][None], w_down[l].astype(BF16), final_g[None], 512)
```

```python
import functools

import jax
import jax.numpy as jnp
from jax import lax
from jax.experimental import pallas as pl
from jax.experimental.pallas import tpu as pltpu

F32 = jnp.float32
BF16 = jnp.bfloat16

D_MODEL = 1024
GRID_W = 64
EPS = 1e-6
N_MOD = 6
D_SSD = 2048
SSD_HEADDIM = 64
SSD_HEADS = 32
SSD_GROUPS = 4
SSD_HPG = 8
SSD_STATE = 128
SSD_CHUNK = 128
D_BC = SSD_GROUPS * SSD_STATE
D_XBC = D_SSD + 2 * D_BC
ATT_HEADS = 16
ATT_KV_HEADS = 4
ATT_GROUP = 4
ATT_HEADDIM = 64
ATT_BLOCK = 128
D_Q = ATT_HEADS * ATT_HEADDIM
D_KV = ATT_KV_HEADS * ATT_HEADDIM
ROPE_BASE = 10000.0
D_FF = 2816
Z0 = 0
XBC0 = Z0 + D_SSD
DT0 = XBC0 + D_XBC
Q0 = DT0 + 2 * SSD_HEADS
K0 = Q0 + D_Q
V0 = K0 + D_KV
G0 = V0 + D_KV
D_IN = G0 + 2 * D_MODEL

V7X_VMEM_BYTES = 64 * 1024 * 1024
LANES = 128
SUBLANES = 8
HALO = SUBLANES
NEG_BIG = -1e30
LOG2E = 1.4426950408889634

INPROJ_ROWS = 256
MERGE_ROWS = 512
FFN_ROWS = 512
SSD_CHUNKS_PER_STEP = 4


def _vmem_limit(nbytes):
    return int(min(nbytes, V7X_VMEM_BYTES - 8 * 1024 * 1024))


def _resident(shape, index=None):
    index = (0,) * len(shape) if index is None else tuple(index)
    return pl.BlockSpec(shape, lambda *_: index, pipeline_mode=pl.Buffered(1))


def _sigmoid(x):
    return 1.0 / (1.0 + jnp.exp2(x * (-LOG2E)))


def _silu(x):
    return x * _sigmoid(x)


def _dot(a, b):
    return jnp.dot(a, b, preferred_element_type=F32)


def _dot_nt(a, b):
    return lax.dot_general(a, b, (((1,), (1,)), ((), ())), preferred_element_type=F32)


def _modulated_norm(x, g, shift, scale):
    ms = jnp.mean(x * x, axis=-1, keepdims=True)
    return x * lax.rsqrt(ms + EPS) * g * (1.0 + scale) + shift


def _mod_kernel(c_ref, w_ref, b_ref, o_ref):
    s = _silu(c_ref[...])
    o_ref[...] = _dot(s.astype(BF16), w_ref[...].astype(BF16)) + b_ref[...]


def _mod_call(cc, w_mod, b_mod):
    n = w_mod.shape[1]
    tn = 1536
    return pl.pallas_call(
        _mod_kernel,
        grid=(n // tn,),
        in_specs=[pl.BlockSpec((SUBLANES, D_MODEL), lambda j: (0, 0)),
                  pl.BlockSpec((D_MODEL, tn), lambda j: (0, j)),
                  pl.BlockSpec((1, tn), lambda j: (0, j))],
        out_specs=pl.BlockSpec((SUBLANES, tn), lambda j: (0, j)),
        out_shape=jax.ShapeDtypeStruct((SUBLANES, n), F32),
        compiler_params=pltpu.CompilerParams(dimension_semantics=("arbitrary",),
                                             vmem_limit_bytes=_vmem_limit(40 << 20)),
        name="mod",
    )(cc, w_mod, b_mod)


def _halo_maps(tm, n_rows):
    hb = tm // HALO
    last = n_rows // HALO - 1
    prev_map = lambda b, i: (b, jnp.maximum(i * hb - 1, 0), 0)
    next_map = lambda b, i: (b, jnp.minimum((i + 1) * hb, last), 0)
    return prev_map, next_map


def _conv3_rows(u, tm, w, b):
    n = u.shape[0]
    before = pltpu.roll(u, 1, 0)[0:tm]
    after = pltpu.roll(u, n - 1, 0)[0:tm]
    return before * w[0:1] + u[0:tm] * w[1:2] + after * w[2:3] + b


def _store_normed_rows(h_scr, x_ref, xp_ref, xn_ref, g, shift, scale, tm):
    i = pl.program_id(1)
    n_i = pl.num_programs(1)
    h = _modulated_norm(x_ref[0], g, shift, scale)
    hp = _modulated_norm(xp_ref[0], g, shift, scale) * (i > 0).astype(F32)
    hn = _modulated_norm(xn_ref[0], g, shift, scale) * (i < n_i - 1).astype(F32)
    h_scr[0:tm, :] = h.astype(BF16)
    h_scr[tm:tm + 2 * HALO, :] = jnp.concatenate([hn, hp], axis=0).astype(BF16)


def _rope(t, cos, sin_signed, n_tiles):
    width = t.shape[1]
    lane = lax.broadcasted_iota(jnp.int32, t.shape, 1)
    first = (lane % 32) < 16
    partner = jnp.where(first, pltpu.roll(t, width - 16, 1), pltpu.roll(t, 16, 1))
    cos_t = jnp.concatenate([cos] * n_tiles, axis=1)
    sin_t = jnp.concatenate([sin_signed] * n_tiles, axis=1)
    return t * cos_t + partner * sin_t


XBC_WINDOWS = (512, 2560)
assert sum(XBC_WINDOWS) == D_XBC
QKVG_K0 = D_Q
QKVG_V0 = QKVG_K0 + D_KV
QKVG_G0 = QKVG_V0 + D_KV
QKVG_END = QKVG_G0 + 2 * D_MODEL
P_DT0 = 0
P_B0 = LANES
P_Z0 = P_B0 + QKVG_END


def _inproj_kernel(*refs, tm, latent):
    n_win = len(XBC_WINDOWS)
    x_ref, xp_ref, xn_ref, mod_ref, g_ref = refs[:5]
    wx_refs = refs[5:5 + n_win]
    rest = refs[5 + n_win:]
    if latent:
        (wdt_ref, wb_ref, wz_ref, cw_ref, cb_ref, dtb_ref, am_ref, cos_ref, sin_ref,
         xbc_o, dtla_o, k_o, v_o, z_o, q_o, gate_o, u_scr, p_scr, h_scr) = rest
        pk0, pv0 = P_B0 + QKVG_K0, P_B0 + QKVG_V0
    else:
        (wdt_ref, wb_ref, cw_ref, cb_ref, dtb_ref, am_ref,
         xbc_o, dtla_o, k_o, v_o, u_scr, p_scr, h_scr) = rest
        pk0, pv0 = P_B0, P_B0 + D_KV
    i = pl.program_id(1)

    @pl.when(i >= 0)
    def _():
        _store_normed_rows(h_scr, x_ref, xp_ref, xn_ref, g_ref[...], mod_ref[0, :, 0:D_MODEL],
                           mod_ref[0, :, D_MODEL:2 * D_MODEL], tm)

    c0 = 0
    for wx_ref in wx_refs:
        u_scr[:, c0:c0 + wx_ref.shape[1]] = _dot(h_scr[...], wx_ref[...])
        c0 += wx_ref.shape[1]
    p_scr[:, P_DT0:P_B0] = _dot(h_scr[0:tm, :], wdt_ref[...])
    p_scr[:, P_B0:P_B0 + wb_ref.shape[1]] = _dot(h_scr[0:tm, :], wb_ref[...])
    if latent:
        p_scr[:, P_Z0:P_Z0 + D_SSD] = _dot(h_scr[0:tm, :], wz_ref[...])

    cchunk = 512
    for c0 in range(0, D_XBC, cchunk):
        cols = slice(c0, c0 + cchunk)
        n = tm + 2 * HALO
        before = jnp.concatenate([u_scr[n - 1:n, cols], u_scr[0:tm - 1, cols]], axis=0)
        after = jnp.concatenate([u_scr[1:tm, cols], u_scr[tm:tm + 1, cols]], axis=0)
        y = (before * cw_ref[0:1, cols] + u_scr[0:tm, cols] * cw_ref[1:2, cols]
             + after * cw_ref[2:3, cols] + cb_ref[:, cols])
        xbc_o[0, :, cols] = _silu(y).astype(BF16)

    raw = p_scr[:, P_DT0:P_B0]
    lane = lax.broadcasted_iota(jnp.int32, raw.shape, 1)
    raw = jnp.where(lane < 2 * SSD_HEADS, raw, pltpu.roll(raw, 2 * SSD_HEADS, 1)) + dtb_ref[...]
    dt = jnp.maximum(raw, 0.0) + jnp.log1p(jnp.exp(-jnp.abs(raw)))
    dtla_o[0] = jnp.where(lane < 2 * SSD_HEADS, dt, dt * am_ref[...])

    kf = p_scr[:, pk0:pk0 + D_KV]
    if latent:
        cos = cos_ref[...]
        sin = sin_ref[...]
        kf = _rope(kf, cos, sin, D_KV // LANES)
    k_o[0] = kf.astype(BF16)
    v_o[0] = p_scr[:, pv0:pv0 + D_KV].astype(BF16)
    if latent:
        z_o[0] = _silu(p_scr[:, P_Z0:P_Z0 + D_SSD]).astype(BF16)
        qf = _rope(p_scr[:, P_B0:P_B0 + D_Q], cos, sin, D_Q // LANES)
        q_o[0] = (qf * (ATT_HEADDIM ** -0.5 * LOG2E)).astype(BF16)
        gate_o[0] = _sigmoid(p_scr[:, P_B0 + QKVG_G0:P_B0 + QKVG_END]).astype(BF16)


def _inproj_call(x, modv, per_batch_mod, norm_g, w_in_bf16, w_qkvg, consts, rope, tm, latent):
    bsz, n_rows, _ = x.shape
    prev_map, next_map = _halo_maps(tm, n_rows)
    row_map = lambda b, i: (b, i, 0)
    mod_map = (lambda b, i: (b, 0, 0)) if per_batch_mod else (lambda b, i: (0, 0, 0))
    assert DT0 % LANES == 0
    xbc_specs, start = [], XBC0
    for width in XBC_WINDOWS:
        assert start % width == 0
        xbc_specs.append(_resident((D_MODEL, width), (0, start // width)))
        start += width
    n_win = len(XBC_WINDOWS)
    in_specs = ([pl.BlockSpec((1, tm, D_MODEL), row_map),
                 pl.BlockSpec((1, HALO, D_MODEL), prev_map),
                 pl.BlockSpec((1, HALO, D_MODEL), next_map),
                 pl.BlockSpec((1, 1, N_MOD * D_MODEL), mod_map),
                 _resident((1, D_MODEL))]
                + xbc_specs
                + [_resident((D_MODEL, LANES), (0, DT0 // LANES))])
    args = [x, x, x, modv, norm_g] + [w_in_bf16] * (n_win + 1)
    if latent:
        in_specs += [_resident((D_MODEL, QKVG_END)), _resident((D_MODEL, D_SSD), (0, 0))]
        args += [w_qkvg, w_in_bf16]
    else:
        assert QKVG_K0 % (2 * D_KV) == 0
        in_specs += [_resident((D_MODEL, 2 * D_KV), (0, QKVG_K0 // (2 * D_KV)))]
        args += [w_qkvg]
    in_specs += [_resident((3, D_XBC)), _resident((1, D_XBC)), _resident((1, LANES)), _resident((1, LANES))]
    args += [consts["conv_w"], consts["conv_b"], consts["dt_bias"], consts["a_mult"]]
    out_shape = [jax.ShapeDtypeStruct((bsz, n_rows, D_XBC), BF16),
                 jax.ShapeDtypeStruct((bsz, n_rows, LANES), F32),
                 jax.ShapeDtypeStruct((bsz, n_rows, D_KV), BF16),
                 jax.ShapeDtypeStruct((bsz, n_rows, D_KV), BF16)]
    out_specs = [pl.BlockSpec((1, tm, D_XBC), row_map), pl.BlockSpec((1, tm, LANES), row_map),
                 pl.BlockSpec((1, tm, D_KV), row_map), pl.BlockSpec((1, tm, D_KV), row_map)]
    if latent:
        in_specs += [pl.BlockSpec((tm, LANES), lambda b, i: (i, 0)),
                     pl.BlockSpec((tm, LANES), lambda b, i: (i, 0))]
        args += [rope[0], rope[1]]
        out_shape += [jax.ShapeDtypeStruct((bsz, n_rows, D_SSD), BF16),
                      jax.ShapeDtypeStruct((bsz, n_rows, D_Q), BF16),
                      jax.ShapeDtypeStruct((bsz, n_rows, 2 * D_MODEL), BF16)]
        out_specs += [pl.BlockSpec((1, tm, D_SSD), row_map), pl.BlockSpec((1, tm, D_Q), row_map),
                      pl.BlockSpec((1, tm, 2 * D_MODEL), row_map)]
    return pl.pallas_call(
        functools.partial(_inproj_kernel, tm=tm, latent=latent),
        grid=(bsz, n_rows // tm),
        in_specs=in_specs, out_specs=out_specs, out_shape=out_shape,
        scratch_shapes=[pltpu.VMEM((tm + 2 * HALO, D_XBC), F32),
                        pltpu.VMEM((tm, (P_Z0 + D_SSD) if latent else (P_B0 + 2 * D_KV)), F32),
                        pltpu.VMEM((tm + 2 * HALO, D_MODEL), BF16)],
        compiler_params=pltpu.CompilerParams(dimension_semantics=("arbitrary", "arbitrary"),
                                             vmem_limit_bytes=_vmem_limit(56 << 20)),
        name="inproj_lat" if latent else "inproj_ctx",
    )(*args)


QUAD = 4
QW = QUAD * SSD_HEADDIM


def _ssd_kernel(*refs, reverse, has_h0, with_y, fuse_out, cps):
    T = SSD_CHUNK
    it = iter(refs)
    xbc_ref = next(it)
    dtla_ref = next(it)
    h0_ref = next(it) if has_h0 else None
    if fuse_out:
        yf_ref, z_ref, dskip_ref, ng_ref = next(it), next(it), next(it), next(it)
    y_o = next(it) if with_y else None
    hfin_o = next(it)
    s_scr = next(it)
    rhs_scr = next(it)
    lhs_scr = next(it)
    y_scr = next(it) if with_y else None

    i = pl.program_id(1)
    n_i = pl.num_programs(1)

    @pl.when(i == 0)
    def _():
        if has_h0:
            s_scr[...] = h0_ref[0]
        else:
            s_scr[...] = jnp.zeros_like(s_scr)
        rhs_scr[...] = jnp.zeros_like(rhs_scr)

    d = 1 if reverse else 0
    dt_c0 = d * SSD_HEADS
    la_c0 = 2 * SSD_HEADS + d * SSD_HEADS
    lane = lax.broadcasted_iota(jnp.int32, (T, LANES), 1)
    row = lax.broadcasted_iota(jnp.int32, (T, T), 0)
    col = lax.broadcasted_iota(jnp.int32, (T, T), 1)
    keep = (col >= row) if reverse else (col <= row)
    tri = keep.astype(F32)
    end_row = 0 if reverse else T - 1
    lane_pair = lax.broadcasted_iota(jnp.int32, (1, LANES), 1) < SSD_HEADDIM
    n_quads = SSD_HEADS // QUAD

    def chunk_constants(ck):
        rows = pl.ds(ck * T, T)
        dtla = dtla_ref[0, rows, :]
        dtla2 = jnp.where(lane >= 2 * SSD_HEADS, dtla * LOG2E, dtla)
        cs = jnp.dot(tri, dtla2, preferred_element_type=F32, precision=lax.Precision.HIGHEST)
        cs_t = cs.T
        dtla_t = dtla.T
        a_t = cs_t[la_c0:la_c0 + SSD_HEADS]
        a_end_t = cs_t[la_c0:la_c0 + SSD_HEADS, end_row:end_row + 1]
        dt_t = dtla_t[dt_c0:dt_c0 + SSD_HEADS]
        w_t = dt_t * jnp.exp2(a_end_t - a_t)
        adt_t = a_t - jnp.log2(dt_t)
        groups = []
        for g in range(SSD_GROUPS):
            b_g = xbc_ref[0, rows, D_SSD + g * SSD_STATE:D_SSD + (g + 1) * SSD_STATE]
            bt_g = b_g.astype(F32).T
            c_g = cb_g = None
            if with_y:
                c_g = xbc_ref[0, rows, D_SSD + D_BC + g * SSD_STATE:D_SSD + D_BC + (g + 1) * SSD_STATE]
                cb_g = _dot_nt(c_g, b_g)
            groups.append((bt_g, c_g, cb_g))
        return cs, w_t, adt_t, groups

    def scan_chunk(ck, constants):
        rows = pl.ds(ck * T, T)
        cs, w_t, adt_t, group_consts = constants

        def y_put(pc0, val):
            y_scr[ck, :, pc0:pc0 + LANES] = val

        def y_get(pc0):
            return y_scr[ck, :, pc0:pc0 + LANES]

        boff = T if with_y else 0
        groups = []
        for g, (bt_g, c_g, cb_g) in enumerate(group_consts):
            yoff_g = None
            if with_y:
                gc0 = g * SSD_HPG * SSD_HEADDIM
                s_g = s_scr[:, gc0:gc0 + SSD_HPG * SSD_HEADDIM].astype(BF16)
                yoff_g = _dot(c_g, s_g)
            groups.append((bt_g, cb_g, yoff_g))

        def prep(q):
            g = q // (SSD_HPG // QUAD)
            bt_g, cb_g, yoff_g = groups[g]
            gc0 = g * SSD_HPG * SSD_HEADDIM
            h0 = q * QUAD
            c0 = h0 * SSD_HEADDIM
            slot = ck * n_quads + q
            for j in range(QUAD):
                rhs_scr[slot, pl.ds(j * T, T), j * SSD_HEADDIM:(j + 1) * SSD_HEADDIM] = (
                    xbc_ref[0, rows, c0 + j * SSD_HEADDIM:c0 + (j + 1) * SSD_HEADDIM])
            decays = []
            ar_even = None
            for j in range(QUAD):
                hh = h0 + j
                lhs_scr[slot, boff:boff + SSD_STATE, j * T:(j + 1) * T] = (
                    bt_g * w_t[hh:hh + 1]).astype(BF16)
                ar = jnp.broadcast_to(cs[:, la_c0 + hh:la_c0 + hh + 1], (T, LANES))
                if with_y:
                    seg = jnp.where(keep, ar - adt_t[hh:hh + 1], NEG_BIG)
                    lhs_scr[slot, 0:T, j * T:(j + 1) * T] = (cb_g * jnp.exp2(seg)).astype(BF16)
                if j % 2 == 0:
                    ar_even = ar
                    continue
                ar_pair = jnp.where(lane_pair, ar_even, ar)
                pc0 = c0 + (j // 2) * LANES
                if with_y:
                    yo = yoff_g[:, pc0 - gc0:pc0 - gc0 + LANES]
                    y_put(pc0, yo * jnp.exp2(ar_pair))
                decays.append(jnp.exp2(ar_pair[end_row:end_row + 1]))
            return decays

        def finish(q, decays):
            slot = ck * n_quads + q
            c0 = q * QUAD * SSD_HEADDIM
            res = _dot(lhs_scr[slot], rhs_scr[slot])
            for p in range(QUAD // 2):
                pc0 = c0 + p * LANES
                if with_y:
                    y_pair = y_get(pc0) + res[:T, p * LANES:(p + 1) * LANES]
                    if fuse_out:
                        y_put(pc0, y_pair)
                    else:
                        y_o[0, rows, pc0:pc0 + LANES] = y_pair.astype(y_o.dtype)
                s_scr[:, pc0:pc0 + LANES] = (s_scr[:, pc0:pc0 + LANES] * decays[p]
                                             + res[boff:, p * LANES:(p + 1) * LANES])

        for q in range(n_quads):
            finish(q, prep(q))

        if fuse_out:
            xs = xbc_ref[0, rows, 0:D_SSD].astype(F32)
            y = y_scr[ck] + yf_ref[0, rows, :].astype(F32) + dskip_ref[...] * xs
            gy = y * z_ref[0, rows, :].astype(F32)
            ms = jnp.mean(gy * gy, axis=-1, keepdims=True)
            y_o[0, rows, :] = (gy * lax.rsqrt(ms + EPS) * ng_ref[...]).astype(y_o.dtype)

    order = list(range(cps - 1, -1, -1) if reverse else range(cps))
    constants = {ck: chunk_constants(ck) for ck in order}
    for ck in order:
        scan_chunk(ck, constants[ck])

    @pl.when(i == n_i - 1)
    def _():
        hfin_o[0] = s_scr[...]


def _ssd_call(xbc, dtla, h0, reverse, with_y, fuse=None, cps=SSD_CHUNKS_PER_STEP):
    bsz, n_rows, _ = xbc.shape
    T = SSD_CHUNK
    cps = min(cps, n_rows // T)
    rows = cps * T
    ns = n_rows // rows
    cmap = (lambda b, i: (b, ns - 1 - i, 0)) if reverse else (lambda b, i: (b, i, 0))
    bmap = lambda b, i: (b, 0, 0)
    in_specs = [pl.BlockSpec((1, rows, D_XBC), cmap), pl.BlockSpec((1, rows, LANES), cmap)]
    args = [xbc, dtla]
    if h0 is not None:
        in_specs.append(pl.BlockSpec((1, SSD_STATE, D_SSD), bmap))
        args.append(h0)
    if fuse is not None:
        yf, zg, dskip, ng = fuse
        in_specs += [pl.BlockSpec((1, rows, D_SSD), cmap), pl.BlockSpec((1, rows, D_SSD), cmap),
                     _resident((1, D_SSD)), _resident((1, D_SSD))]
        args += [yf, zg, dskip, ng]
    out_shape, out_specs = [], []
    if with_y:
        out_shape.append(jax.ShapeDtypeStruct((bsz, n_rows, D_SSD), BF16))
        out_specs.append(pl.BlockSpec((1, rows, D_SSD), cmap))
    out_shape.append(jax.ShapeDtypeStruct((bsz, SSD_STATE, D_SSD), F32))
    out_specs.append(pl.BlockSpec((1, SSD_STATE, D_SSD), bmap))
    scratch = [pltpu.VMEM((SSD_STATE, D_SSD), F32),
               pltpu.VMEM((cps * SSD_HEADS // QUAD, QUAD * T, QW), BF16),
               pltpu.VMEM((cps * SSD_HEADS // QUAD, (T if with_y else 0) + SSD_STATE, QUAD * T), BF16)]
    if with_y:
        scratch.append(pltpu.VMEM((cps, T, D_SSD), F32))
    name = "ssd_" + ("rev" if reverse else "fwd") + ("_y" if with_y else "_state") + ("_out" if fuse else "")
    return pl.pallas_call(
        functools.partial(_ssd_kernel, reverse=reverse, has_h0=h0 is not None, with_y=with_y,
                          fuse_out=fuse is not None, cps=cps),
        grid=(bsz, ns),
        in_specs=in_specs, out_specs=out_specs, out_shape=out_shape,
        scratch_shapes=scratch,
        compiler_params=pltpu.CompilerParams(dimension_semantics=("arbitrary", "arbitrary"),
                                             vmem_limit_bytes=_vmem_limit(52 << 20)),
        name=name,
    )(*args)


ATT_STRIP = 16
ATT_QB = 4
ATT_AHEAD = 1


def _attn_kernel(q_ref, kp_ref, kc_ref, kn_ref, vp_ref, vc_ref, vn_ref, kx_ref, vx_ref, sink_ref, o_ref,
                 s_scr, p_scr, bias_scr):
    blk = ATT_BLOCK
    n = pl.program_id(1)
    n_n = pl.num_programs(1)
    row = lax.broadcasted_iota(jnp.int32, (blk, blk), 0)
    col = lax.broadcasted_iota(jnp.int32, (blk, blk), 1)
    lane_kv = lax.broadcasted_iota(jnp.int32, (1, D_KV), 1) // ATT_HEADDIM
    kv_masks = [(lane_kv == j) for j in range(ATT_KV_HEADS)]
    kwin = [kp_ref[0]] + [kc_ref[0, sb * blk:(sb + 1) * blk, :] for sb in range(ATT_QB)] + [kn_ref[0]]
    vwin = [vp_ref[0]] + [vc_ref[0, sb * blk:(sb + 1) * blk, :] for sb in range(ATT_QB)] + [vn_ref[0]]
    kcats, vcats = [], []
    for sb in range(ATT_QB):
        prev_ok = (col >= row) if sb > 0 else jnp.logical_and(col >= row, n > 0)
        next_ok = (col <= row) if sb < ATT_QB - 1 else jnp.logical_and(col <= row, n < n_n - 1)
        bias_scr[2 * sb] = jnp.where(prev_ok, 0.0, NEG_BIG)
        bias_scr[2 * sb + 1] = jnp.where(next_ok, 0.0, NEG_BIG)
        kcats.append(jnp.concatenate(kwin[sb:sb + 3] + [kx_ref[0]], axis=0))
        vcats.append(jnp.concatenate(vwin[sb:sb + 3] + [vx_ref[0]], axis=0))
    nk = kcats[0].shape[0]
    pairs = [(sb, g) for sb in range(ATT_QB) for g in range(ATT_GROUP)]

    def scores(sb, g):
        qg = q_ref[0, sb * blk:(sb + 1) * blk, g * D_KV:(g + 1) * D_KV]
        qs = jnp.concatenate([jnp.where(kv_masks[j], qg, jnp.zeros_like(qg))
                              for j in range(ATT_KV_HEADS)], axis=0)
        s_scr[sb * ATT_GROUP + g] = _dot_nt(qs, kcats[sb])

    for sb, g in pairs[:ATT_AHEAD]:
        scores(sb, g)
    for idx, (sb, g) in enumerate(pairs):
        if idx + ATT_AHEAD < len(pairs):
            scores(*pairs[idx + ATT_AHEAD])
        slot = sb * ATT_GROUP + g
        vcat = vcats[sb]
        for st in range(ATT_KV_HEADS * blk // ATT_STRIP):
            r0 = st * ATT_STRIP
            i0 = r0 % blk
            sink = sink_ref[(r0 // blk) * ATT_GROUP + g] * LOG2E
            rows = pl.ds(r0, ATT_STRIP)
            parts = [s_scr[slot, rows, 0:blk] + bias_scr[2 * sb, pl.ds(i0, ATT_STRIP), :],
                     s_scr[slot, rows, blk:2 * blk],
                     s_scr[slot, rows, 2 * blk:3 * blk] + bias_scr[2 * sb + 1, pl.ds(i0, ATT_STRIP), :]]
            parts += [s_scr[slot, rows, c0:c0 + LANES] for c0 in range(3 * blk, nk, LANES)]
            mel = parts[0]
            for t in parts[1:]:
                mel = jnp.maximum(mel, t)
            m = jnp.maximum(jnp.max(mel, axis=-1, keepdims=True), sink)
            ps = [jnp.exp2(t - m) for t in parts]
            tot = ps[0]
            for t in ps[1:]:
                tot = tot + t
            den = jnp.sum(tot, axis=-1, keepdims=True) + jnp.exp2(sink - m)
            inv = 1.0 / den
            for ci, t in enumerate(ps):
                p_scr[slot, rows, ci * LANES:(ci + 1) * LANES] = (t * inv).astype(BF16)
        o = _dot(p_scr[slot], vcat)
        og = jnp.zeros((blk, D_KV), F32)
        for j in range(ATT_KV_HEADS):
            og = jnp.where(kv_masks[j], o[j * blk:(j + 1) * blk], og)
        o_ref[0, sb * blk:(sb + 1) * blk, g * D_KV:(g + 1) * D_KV] = og.astype(o_ref.dtype)


def _attn_call(q, k, v, kx, vx, sink):
    bsz, n_rows, _ = q.shape
    blk = ATT_BLOCK
    nb = n_rows // blk
    lc = kx.shape[1]
    cur = lambda b, n: (b, n, 0)
    prev = lambda b, n: (b, jnp.maximum(n * ATT_QB - 1, 0), 0)
    nxt = lambda b, n: (b, jnp.minimum((n + 1) * ATT_QB, nb - 1), 0)
    ctx = lambda b, n: (b, 0, 0)
    edge_spec = lambda m: pl.BlockSpec((1, blk, D_KV), m)
    cur_spec = pl.BlockSpec((1, ATT_QB * blk, D_KV), cur)
    n_slots = ATT_QB * ATT_GROUP
    return pl.pallas_call(
        _attn_kernel,
        grid=(bsz, nb // ATT_QB),
        in_specs=[pl.BlockSpec((1, ATT_QB * blk, D_Q), cur),
                  edge_spec(prev), cur_spec, edge_spec(nxt),
                  edge_spec(prev), cur_spec, edge_spec(nxt),
                  pl.BlockSpec((1, lc, D_KV), ctx), pl.BlockSpec((1, lc, D_KV), ctx),
                  pl.BlockSpec(memory_space=pltpu.SMEM)],
        out_specs=pl.BlockSpec((1, ATT_QB * blk, D_Q), cur),
        out_shape=jax.ShapeDtypeStruct((bsz, n_rows, D_Q), BF16),
        scratch_shapes=[pltpu.VMEM((n_slots, ATT_KV_HEADS * blk, 3 * blk + lc), F32),
                        pltpu.VMEM((n_slots, ATT_KV_HEADS * blk, 3 * blk + lc), BF16),
                        pltpu.VMEM((2 * ATT_QB, blk, blk), F32)],
        compiler_params=pltpu.CompilerParams(dimension_semantics=("arbitrary", "arbitrary"),
                                             vmem_limit_bytes=_vmem_limit(48 << 20)),
        name="attn",
    )(q, k, k, k, v, v, v, kx, vx, sink)


def _merge_kernel(y_ref, a_ref, gate_ref, x_ref, mod_ref, wos_ref, woa_ref, wout_ref, o_ref):
    gates = gate_ref[0].astype(F32)
    m = (gates[:, :D_MODEL] * _dot(y_ref[0], wos_ref[...])
         + gates[:, D_MODEL:] * _dot(a_ref[0], woa_ref[...]))
    o = _dot(m.astype(BF16), wout_ref[...])
    gt1 = mod_ref[0, :, 2 * D_MODEL:3 * D_MODEL]
    o_ref[0] = x_ref[0] + gt1 * o


def _merge_call(yn, att, gates, x, modv, w_os, w_oa, w_out, tm):
    bsz, n_rows, _ = x.shape
    row_map = lambda b, i: (b, i, 0)
    return pl.pallas_call(
        _merge_kernel,
        grid=(bsz, n_rows // tm),
        in_specs=[pl.BlockSpec((1, tm, D_SSD), row_map), pl.BlockSpec((1, tm, D_Q), row_map),
                  pl.BlockSpec((1, tm, 2 * D_MODEL), row_map), pl.BlockSpec((1, tm, D_MODEL), row_map),
                  pl.BlockSpec((1, 1, N_MOD * D_MODEL), lambda b, i: (b, 0, 0)),
                  _resident((D_SSD, D_MODEL)), _resident((D_Q, D_MODEL)), _resident((D_MODEL, D_MODEL))],
        out_specs=pl.BlockSpec((1, tm, D_MODEL), row_map),
        out_shape=jax.ShapeDtypeStruct((bsz, n_rows, D_MODEL), F32),
        compiler_params=pltpu.CompilerParams(dimension_semantics=("arbitrary", "arbitrary"),
                                             vmem_limit_bytes=_vmem_limit(48 << 20)),
        name="merge",
    )(yn, att, gates, x, modv, w_os, w_oa, w_out)


FF_CHUNK = 256


def _ffn_kernel(x_ref, xp_ref, xn_ref, mod_ref, g_ref, wup_ref, cw_ref, cb_ref, wdn_ref, fg_ref,
                o_ref, ua_scr, ub_scr, act_scr, *, tm):
    i = pl.program_id(1)
    n_i = pl.num_programs(1)
    g = g_ref[...]
    shift = mod_ref[0, :, 3 * D_MODEL:4 * D_MODEL]
    scale = mod_ref[0, :, 4 * D_MODEL:5 * D_MODEL]
    gt2 = mod_ref[0, :, 5 * D_MODEL:6 * D_MODEL]
    x = x_ref[0]
    h = _modulated_norm(x, g, shift, scale)
    hp = _modulated_norm(xp_ref[0], g, shift, scale) * (i > 0).astype(F32)
    hn = _modulated_norm(xn_ref[0], g, shift, scale) * (i < n_i - 1).astype(F32)
    h_ext = jnp.concatenate([h, hn, hp], axis=0).astype(BF16)
    for c in range(D_FF // FF_CHUNK):
        c0 = c * FF_CHUNK
        ua_scr[c] = _dot(h_ext, wup_ref[:, c0:c0 + FF_CHUNK])
        ub_scr[c] = _dot(h_ext, wup_ref[:, D_FF + c0:D_FF + c0 + FF_CHUNK])
    for c in range(D_FF // FF_CHUNK):
        c0 = c * FF_CHUNK
        b0 = D_FF + c0
        a = _conv3_rows(ua_scr[c], tm, cw_ref[:, c0:c0 + FF_CHUNK], cb_ref[:, c0:c0 + FF_CHUNK])
        b = _conv3_rows(ub_scr[c], tm, cw_ref[:, b0:b0 + FF_CHUNK], cb_ref[:, b0:b0 + FF_CHUNK])
        act_scr[:, c0:c0 + FF_CHUNK] = (_silu(a) * b).astype(BF16)
    f = _dot(act_scr[...], wdn_ref[...])
    x2 = x + gt2 * f
    ms = jnp.mean(x2 * x2, axis=-1, keepdims=True)
    o_ref[0] = x2 * lax.rsqrt(ms + EPS) * fg_ref[...]


def _ffn_call(x1, modv, norm_g, w_up, conv_w, conv_b, w_down, final_g, tm):
    bsz, n_rows, _ = x1.shape
    prev_map, next_map = _halo_maps(tm, n_rows)
    row_map = lambda b, i: (b, i, 0)
    return pl.pallas_call(
        functools.partial(_ffn_kernel, tm=tm),
        grid=(bsz, n_rows // tm),
        in_specs=[pl.BlockSpec((1, tm, D_MODEL), row_map),
                  pl.BlockSpec((1, HALO, D_MODEL), prev_map),
                  pl.BlockSpec((1, HALO, D_MODEL), next_map),
                  pl.BlockSpec((1, 1, N_MOD * D_MODEL), lambda b, i: (b, 0, 0)),
                  _resident((1, D_MODEL)),
                  _resident((D_MODEL, 2 * D_FF)), _resident((3, 2 * D_FF)), _resident((1, 2 * D_FF)),
                  _resident((D_FF, D_MODEL)), _resident((1, D_MODEL))],
        out_specs=pl.BlockSpec((1, tm, D_MODEL), row_map),
        out_shape=jax.ShapeDtypeStruct((bsz, n_rows, D_MODEL), F32),
        scratch_shapes=[pltpu.VMEM((D_FF // FF_CHUNK, tm + 2 * HALO, FF_CHUNK), F32),
                        pltpu.VMEM((D_FF // FF_CHUNK, tm + 2 * HALO, FF_CHUNK), F32),
                        pltpu.VMEM((tm, D_FF), BF16)],
        compiler_params=pltpu.CompilerParams(dimension_semantics=("arbitrary", "arbitrary"),
                                             vmem_limit_bytes=_vmem_limit(48 << 20)),
        name="ffn",
    )(x1, x1, x1, modv, norm_g, w_up, conv_w, conv_b, w_down, final_g)


def _rope_tables(n_rows):
    half = ATT_HEADDIM // 2
    inv = ROPE_BASE ** (-jnp.arange(0, half, 2, dtype=F32) / half)
    pos = jnp.arange(n_rows)
    rowp = (pos // GRID_W).astype(F32)[:, None] * inv[None]
    colp = (pos % GRID_W).astype(F32)[:, None] * inv[None]
    cos = jnp.concatenate([jnp.cos(rowp), jnp.cos(rowp), jnp.cos(colp), jnp.cos(colp)], axis=-1)
    sin = jnp.concatenate([-jnp.sin(rowp), jnp.sin(rowp), -jnp.sin(colp), jnp.sin(colp)], axis=-1)
    reps = LANES // ATT_HEADDIM
    return jnp.tile(cos, (1, reps)), jnp.tile(sin, (1, reps))


def kernel(x, c, ctx, c_ctx, w_mod, b_mod, norm1_g, norm2_g, w_in, ssd_conv_w, ssd_conv_b, ssd_dt_bias,
           ssd_a_log, ssd_d, ssd_norm_g, w_o_ssd, w_o_att, att_sink, w_out, w_up, ffn_conv_w, ffn_conv_b,
           w_down, final_g):
    depth = w_mod.shape[0]
    assert depth == 1, "single-layer block"
    l = 0
    bsz, n_lat, _ = x.shape
    assert bsz + 1 <= SUBLANES

    cc = jnp.concatenate([c, c_ctx[None], jnp.zeros((SUBLANES - bsz - 1, D_MODEL), F32)], axis=0)
    mod_all = _mod_call(cc, w_mod[l], b_mod[l][None])
    mod_lat = mod_all[:bsz, None, :]
    mod_ctx = mod_all[bsz:bsz + 1, None, :]

    wi = w_in[l].astype(BF16)
    w_q = (wi[:, Q0:K0].reshape(D_MODEL, ATT_KV_HEADS, ATT_GROUP, ATT_HEADDIM)
           .transpose(0, 2, 1, 3).reshape(D_MODEL, D_Q))
    w_qkvg = jnp.concatenate([w_q, wi[:, K0:D_IN]], axis=1)
    dtb = ssd_dt_bias[l].reshape(1, 2 * SSD_HEADS)
    a_mult = -jnp.exp(ssd_a_log[l].reshape(1, 2 * SSD_HEADS))
    consts = {"conv_w": ssd_conv_w[l], "conv_b": ssd_conv_b[l][None],
              "dt_bias": jnp.concatenate([dtb, dtb], axis=1),
              "a_mult": jnp.concatenate([jnp.ones_like(a_mult), a_mult], axis=1)}
    rope = _rope_tables(n_lat)
    g1 = norm1_g[l][None]

    xbc_c, dtla_c, k_c, v_c = _inproj_call(ctx, mod_ctx, False, g1, wi, w_qkvg, consts, None,
                                           ctx.shape[1], False)
    xbc_l, dtla_l, k_l, v_l, zg, q_l, gates = _inproj_call(x, mod_lat, True, g1, wi, w_qkvg, consts, rope,
                                                           INPROJ_ROWS, True)

    (hf_c,) = _ssd_call(xbc_c, dtla_c, None, False, False)
    (hb_c,) = _ssd_call(xbc_c, dtla_c, None, True, False)
    yf, _ = _ssd_call(xbc_l, dtla_l, hf_c, False, True)
    dskip = jnp.repeat(ssd_d[l], SSD_HEADDIM)[None]
    yn, _ = _ssd_call(xbc_l, dtla_l, hb_c, True, True, fuse=(yf, zg, dskip, ssd_norm_g[l][None]))

    sink = att_sink[l].astype(F32)
    att = _attn_call(q_l, k_l, v_l, k_c, v_c, sink)

    w_oa = (w_o_att[l].reshape(ATT_KV_HEADS, ATT_GROUP, ATT_HEADDIM, D_MODEL)
            .transpose(1, 0, 2, 3).reshape(D_Q, D_MODEL))
    x1 = _merge_call(yn, att, gates, x, mod_lat, w_o_ssd[l].astype(BF16),
                     w_oa.astype(BF16), w_out[l].astype(BF16), MERGE_ROWS)
    return _ffn_call(x1, mod_lat, norm2_g[l][None], w_up[l].astype(BF16), ffn_conv_w[l],
                     ffn_conv_b[l][None], w_down[l].astype(BF16), final_g[None], FFN_ROWS)
```

```python
import functools

import jax
import jax.numpy as jnp
from jax import lax
from jax.experimental import pallas as pl
from jax.experimental.pallas import tpu as pltpu

F32 = jnp.float32
BF16 = jnp.bfloat16

D_MODEL = 1024
GRID_W = 64
EPS = 1e-6
N_MOD = 6
D_SSD = 2048
SSD_HEADDIM = 64
SSD_HEADS = 32
SSD_GROUPS = 4
SSD_HPG = 8
SSD_STATE = 128
SSD_CHUNK = 128
D_BC = SSD_GROUPS * SSD_STATE
D_XBC = D_SSD + 2 * D_BC
ATT_HEADS = 16
ATT_KV_HEADS = 4
ATT_GROUP = 4
ATT_HEADDIM = 64
ATT_BLOCK = 128
D_Q = ATT_HEADS * ATT_HEADDIM
D_KV = ATT_KV_HEADS * ATT_HEADDIM
ROPE_BASE = 10000.0
D_FF = 2816
Z0 = 0
XBC0 = Z0 + D_SSD
DT0 = XBC0 + D_XBC
Q0 = DT0 + 2 * SSD_HEADS
K0 = Q0 + D_Q
V0 = K0 + D_KV
G0 = V0 + D_KV
D_IN = G0 + 2 * D_MODEL

V7X_VMEM_BYTES = 64 * 1024 * 1024
LANES = 128
SUBLANES = 8
HALO = SUBLANES
NEG_BIG = -1e30
LOG2E = 1.4426950408889634

INPROJ_ROWS = 256
MERGE_ROWS = 512
FFN_ROWS = 512
SSD_CHUNKS_PER_STEP = 4


def _vmem_limit(nbytes):
    return int(min(nbytes, V7X_VMEM_BYTES - 8 * 1024 * 1024))


def _resident(shape, index=None):
    index = (0,) * len(shape) if index is None else tuple(index)
    return pl.BlockSpec(shape, lambda *_: index, pipeline_mode=pl.Buffered(1))


def _sigmoid(x):
    return 1.0 / (1.0 + jnp.exp2(x * (-LOG2E)))


def _silu(x):
    return x * _sigmoid(x)


def _dot(a, b):
    return jnp.dot(a, b, preferred_element_type=F32)


def _dot_nt(a, b):
    return lax.dot_general(a, b, (((1,), (1,)), ((), ())), preferred_element_type=F32)


def _modulated_norm(x, g, shift, scale):
    ms = jnp.mean(x * x, axis=-1, keepdims=True)
    return x * lax.rsqrt(ms + EPS) * g * (1.0 + scale) + shift


def _mod_kernel(c_ref, w_ref, b_ref, o_ref):
    s = _silu(c_ref[...])
    o_ref[...] = _dot(s.astype(BF16), w_ref[...].astype(BF16)) + b_ref[...]


def _mod_call(cc, w_mod, b_mod):
    n = w_mod.shape[1]
    tn = 1536
    return pl.pallas_call(
        _mod_kernel,
        grid=(n // tn,),
        in_specs=[pl.BlockSpec((SUBLANES, D_MODEL), lambda j: (0, 0)),
                  pl.BlockSpec((D_MODEL, tn), lambda j: (0, j)),
                  pl.BlockSpec((1, tn), lambda j: (0, j))],
        out_specs=pl.BlockSpec((SUBLANES, tn), lambda j: (0, j)),
        out_shape=jax.ShapeDtypeStruct((SUBLANES, n), F32),
        compiler_params=pltpu.CompilerParams(dimension_semantics=("arbitrary",),
                                             vmem_limit_bytes=_vmem_limit(40 << 20)),
        name="mod",
    )(cc, w_mod, b_mod)


def _halo_maps(tm, n_rows):
    hb = tm // HALO
    last = n_rows // HALO - 1
    prev_map = lambda b, i: (b, jnp.maximum(i * hb - 1, 0), 0)
    next_map = lambda b, i: (b, jnp.minimum((i + 1) * hb, last), 0)
    return prev_map, next_map


def _conv3_rows(u, tm, w, b):
    n = u.shape[0]
    before = pltpu.roll(u, 1, 0)[0:tm]
    after = pltpu.roll(u, n - 1, 0)[0:tm]
    return before * w[0:1] + u[0:tm] * w[1:2] + after * w[2:3] + b


def _store_normed_rows(h_scr, x_ref, xp_ref, xn_ref, g, shift, scale, tm):
    i = pl.program_id(1)
    n_i = pl.num_programs(1)
    h = _modulated_norm(x_ref[0], g, shift, scale)
    hp = _modulated_norm(xp_ref[0], g, shift, scale) * (i > 0).astype(F32)
    hn = _modulated_norm(xn_ref[0], g, shift, scale) * (i < n_i - 1).astype(F32)
    h_scr[0:tm, :] = h.astype(BF16)
    h_scr[tm:tm + 2 * HALO, :] = jnp.concatenate([hn, hp], axis=0).astype(BF16)


def _rope(t, cos, sin_signed, n_tiles):
    width = t.shape[1]
    lane = lax.broadcasted_iota(jnp.int32, t.shape, 1)
    first = (lane % 32) < 16
    partner = jnp.where(first, pltpu.roll(t, width - 16, 1), pltpu.roll(t, 16, 1))
    cos_t = jnp.concatenate([cos] * n_tiles, axis=1)
    sin_t = jnp.concatenate([sin_signed] * n_tiles, axis=1)
    return t * cos_t + partner * sin_t


XBC_WINDOWS = (512, 2560)
assert sum(XBC_WINDOWS) == D_XBC
QKVG_K0 = D_Q
QKVG_V0 = QKVG_K0 + D_KV
QKVG_G0 = QKVG_V0 + D_KV
QKVG_END = QKVG_G0 + 2 * D_MODEL
P_DT0 = 0
P_B0 = LANES
P_Z0 = P_B0 + QKVG_END


def _inproj_kernel(*refs, tm, latent):
    n_win = len(XBC_WINDOWS)
    x_ref, xp_ref, xn_ref, mod_ref, g_ref = refs[:5]
    wx_refs = refs[5:5 + n_win]
    rest = refs[5 + n_win:]
    if latent:
        (wdt_ref, wb_ref, wz_ref, cw_ref, cb_ref, dtb_ref, am_ref, cos_ref, sin_ref,
         xbc_o, dtla_o, k_o, v_o, z_o, q_o, gate_o, u_scr, p_scr, h_scr) = rest
        pk0, pv0 = P_B0 + QKVG_K0, P_B0 + QKVG_V0
    else:
        (wdt_ref, wb_ref, cw_ref, cb_ref, dtb_ref, am_ref,
         xbc_o, dtla_o, k_o, v_o, u_scr, p_scr, h_scr) = rest
        pk0, pv0 = P_B0, P_B0 + D_KV
    i = pl.program_id(1)

    @pl.when(i >= 0)
    def _():
        _store_normed_rows(h_scr, x_ref, xp_ref, xn_ref, g_ref[...], mod_ref[0, :, 0:D_MODEL],
                           mod_ref[0, :, D_MODEL:2 * D_MODEL], tm)

    c0 = 0
    for wx_ref in wx_refs:
        u_scr[:, c0:c0 + wx_ref.shape[1]] = _dot(h_scr[...], wx_ref[...])
        c0 += wx_ref.shape[1]
    p_scr[:, P_DT0:P_B0] = _dot(h_scr[0:tm, :], wdt_ref[...])
    p_scr[:, P_B0:P_B0 + wb_ref.shape[1]] = _dot(h_scr[0:tm, :], wb_ref[...])
    if latent:
        p_scr[:, P_Z0:P_Z0 + D_SSD] = _dot(h_scr[0:tm, :], wz_ref[...])

    cchunk = 512
    for c0 in range(0, D_XBC, cchunk):
        cols = slice(c0, c0 + cchunk)
        n = tm + 2 * HALO
        before = jnp.concatenate([u_scr[n - 1:n, cols], u_scr[0:tm - 1, cols]], axis=0)
        after = jnp.concatenate([u_scr[1:tm, cols], u_scr[tm:tm + 1, cols]], axis=0)
        y = (before * cw_ref[0:1, cols] + u_scr[0:tm, cols] * cw_ref[1:2, cols]
             + after * cw_ref[2:3, cols] + cb_ref[:, cols])
        xbc_o[0, :, cols] = _silu(y).astype(BF16)

    raw = p_scr[:, P_DT0:P_B0]
    lane = lax.broadcasted_iota(jnp.int32, raw.shape, 1)
    raw = jnp.where(lane < 2 * SSD_HEADS, raw, pltpu.roll(raw, 2 * SSD_HEADS, 1)) + dtb_ref[...]
    dt = jnp.maximum(raw, 0.0) + jnp.log1p(jnp.exp(-jnp.abs(raw)))
    dtla_o[0] = jnp.where(lane < 2 * SSD_HEADS, dt, dt * am_ref[...])

    kf = p_scr[:, pk0:pk0 + D_KV]
    if latent:
        cos = cos_ref[...]
        sin = sin_ref[...]
        kf = _rope(kf, cos, sin, D_KV // LANES)
    k_o[0] = kf.astype(BF16)
    v_o[0] = p_scr[:, pv0:pv0 + D_KV].astype(BF16)
    if latent:
        z_o[0] = _silu(p_scr[:, P_Z0:P_Z0 + D_SSD]).astype(BF16)
        qf = _rope(p_scr[:, P_B0:P_B0 + D_Q], cos, sin, D_Q // LANES)
        q_o[0] = (qf * (ATT_HEADDIM ** -0.5 * LOG2E)).astype(BF16)
        gate_o[0] = _sigmoid(p_scr[:, P_B0 + QKVG_G0:P_B0 + QKVG_END]).astype(BF16)


def _inproj_call(x, modv, per_batch_mod, norm_g, w_in_bf16, w_qkvg, consts, rope, tm, latent):
    bsz, n_rows, _ = x.shape
    prev_map, next_map = _halo_maps(tm, n_rows)
    row_map = lambda b, i: (b, i, 0)
    mod_map = (lambda b, i: (b, 0, 0)) if per_batch_mod else (lambda b, i: (0, 0, 0))
    assert DT0 % LANES == 0
    xbc_specs, start = [], XBC0
    for width in XBC_WINDOWS:
        assert start % width == 0
        xbc_specs.append(_resident((D_MODEL, width), (0, start // width)))
        start += width
    n_win = len(XBC_WINDOWS)
    in_specs = ([pl.BlockSpec((1, tm, D_MODEL), row_map),
                 pl.BlockSpec((1, HALO, D_MODEL), prev_map),
                 pl.BlockSpec((1, HALO, D_MODEL), next_map),
                 pl.BlockSpec((1, 1, N_MOD * D_MODEL), mod_map),
                 _resident((1, D_MODEL))]
                + xbc_specs
                + [_resident((D_MODEL, LANES), (0, DT0 // LANES))])
    args = [x, x, x, modv, norm_g] + [w_in_bf16] * (n_win + 1)
    if latent:
        in_specs += [_resident((D_MODEL, QKVG_END)), _resident((D_MODEL, D_SSD), (0, 0))]
        args += [w_qkvg, w_in_bf16]
    else:
        assert QKVG_K0 % (2 * D_KV) == 0
        in_specs += [_resident((D_MODEL, 2 * D_KV), (0, QKVG_K0 // (2 * D_KV)))]
        args += [w_qkvg]
    in_specs += [_resident((3, D_XBC)), _resident((1, D_XBC)), _resident((1, LANES)), _resident((1, LANES))]
    args += [consts["conv_w"], consts["conv_b"], consts["dt_bias"], consts["a_mult"]]
    out_shape = [jax.ShapeDtypeStruct((bsz, n_rows, D_XBC), BF16),
                 jax.ShapeDtypeStruct((bsz, n_rows, LANES), F32),
                 jax.ShapeDtypeStruct((bsz, n_rows, D_KV), BF16),
                 jax.ShapeDtypeStruct((bsz, n_rows, D_KV), BF16)]
    out_specs = [pl.BlockSpec((1, tm, D_XBC), row_map), pl.BlockSpec((1, tm, LANES), row_map),
                 pl.BlockSpec((1, tm, D_KV), row_map), pl.BlockSpec((1, tm, D_KV), row_map)]
    if latent:
        in_specs += [pl.BlockSpec((tm, LANES), lambda b, i: (i, 0)),
                     pl.BlockSpec((tm, LANES), lambda b, i: (i, 0))]
        args += [rope[0], rope[1]]
        out_shape += [jax.ShapeDtypeStruct((bsz, n_rows, D_SSD), BF16),
                      jax.ShapeDtypeStruct((bsz, n_rows, D_Q), BF16),
                      jax.ShapeDtypeStruct((bsz, n_rows, 2 * D_MODEL), BF16)]
        out_specs += [pl.BlockSpec((1, tm, D_SSD), row_map), pl.BlockSpec((1, tm, D_Q), row_map),
                      pl.BlockSpec((1, tm, 2 * D_MODEL), row_map)]
    return pl.pallas_call(
        functools.partial(_inproj_kernel, tm=tm, latent=latent),
        grid=(bsz, n_rows // tm),
        in_specs=in_specs, out_specs=out_specs, out_shape=out_shape,
        scratch_shapes=[pltpu.VMEM((tm + 2 * HALO, D_XBC), F32),
                        pltpu.VMEM((tm, (P_Z0 + D_SSD) if latent else (P_B0 + 2 * D_KV)), F32),
                        pltpu.VMEM((tm + 2 * HALO, D_MODEL), BF16)],
        compiler_params=pltpu.CompilerParams(dimension_semantics=("arbitrary", "arbitrary"),
                                             vmem_limit_bytes=_vmem_limit(56 << 20)),
        name="inproj_lat" if latent else "inproj_ctx",
    )(*args)


QUAD = 4
QW = QUAD * SSD_HEADDIM


def _ssd_kernel(*refs, reverse, has_h0, with_y, fuse_out, cps):
    T = SSD_CHUNK
    it = iter(refs)
    xbc_ref = next(it)
    dtla_ref = next(it)
    h0_ref = next(it) if has_h0 else None
    if fuse_out:
        yf_ref, z_ref, dskip_ref, ng_ref = next(it), next(it), next(it), next(it)
    y_o = next(it) if with_y else None
    hfin_o = next(it)
    s_scr = next(it)
    rhs_scr = next(it)
    lhs_scr = next(it)
    y_scr = next(it) if with_y else None

    i = pl.program_id(1)
    n_i = pl.num_programs(1)

    @pl.when(i == 0)
    def _():
        if has_h0:
            s_scr[...] = h0_ref[0]
        else:
            s_scr[...] = jnp.zeros_like(s_scr)
        rhs_scr[...] = jnp.zeros_like(rhs_scr)

    d = 1 if reverse else 0
    dt_c0 = d * SSD_HEADS
    la_c0 = 2 * SSD_HEADS + d * SSD_HEADS
    lane = lax.broadcasted_iota(jnp.int32, (T, LANES), 1)
    row = lax.broadcasted_iota(jnp.int32, (T, T), 0)
    col = lax.broadcasted_iota(jnp.int32, (T, T), 1)
    keep = (col >= row) if reverse else (col <= row)
    tri = jnp.where(keep, 1.0, 0.0).astype(BF16)
    end_row = 0 if reverse else T - 1
    lane_pair = lax.broadcasted_iota(jnp.int32, (1, LANES), 1) < SSD_HEADDIM
    n_quads = SSD_HEADS // QUAD

    def chunk_constants(ck):
        rows = pl.ds(ck * T, T)
        dtla = dtla_ref[0, rows, :]
        dtla2 = jnp.where(lane >= 2 * SSD_HEADS, dtla * LOG2E, dtla)
        hi = dtla2.astype(BF16)
        r1 = dtla2 - hi.astype(F32)
        mid = r1.astype(BF16)
        lo = (r1 - mid.astype(F32)).astype(BF16)
        cs3 = _dot(tri, jnp.concatenate([hi, mid, lo], axis=1))
        cs = cs3[:, 0:LANES] + cs3[:, LANES:2 * LANES] + cs3[:, 2 * LANES:3 * LANES]
        cs_t = cs.T
        dtla_t = dtla.T
        a_t = cs_t[la_c0:la_c0 + SSD_HEADS]
        a_end_t = cs_t[la_c0:la_c0 + SSD_HEADS, end_row:end_row + 1]
        dt_t = dtla_t[dt_c0:dt_c0 + SSD_HEADS]
        w_t = dt_t * jnp.exp2(a_end_t - a_t)
        adt_t = a_t - jnp.log2(dt_t)
        groups = []
        for g in range(SSD_GROUPS):
            b_g = xbc_ref[0, rows, D_SSD + g * SSD_STATE:D_SSD + (g + 1) * SSD_STATE]
            bt_g = b_g.astype(F32).T
            c_g = cb_g = None
            if with_y:
                c_g = xbc_ref[0, rows, D_SSD + D_BC + g * SSD_STATE:D_SSD + D_BC + (g + 1) * SSD_STATE]
                cb_g = _dot_nt(c_g, b_g)
            groups.append((bt_g, c_g, cb_g))
        return cs, w_t, adt_t, groups

    def scan_chunk(ck, constants):
        rows = pl.ds(ck * T, T)
        cs, w_t, adt_t, group_consts = constants

        def y_put(pc0, val):
            y_scr[ck, :, pc0:pc0 + LANES] = val

        def y_get(pc0):
            return y_scr[ck, :, pc0:pc0 + LANES]

        boff = T if with_y else 0
        groups = []
        for g, (bt_g, c_g, cb_g) in enumerate(group_consts):
            yoff_g = None
            if with_y:
                gc0 = g * SSD_HPG * SSD_HEADDIM
                s_g = s_scr[:, gc0:gc0 + SSD_HPG * SSD_HEADDIM].astype(BF16)
                yoff_g = _dot(c_g, s_g)
            groups.append((bt_g, cb_g, yoff_g))

        def prep(q):
            g = q // (SSD_HPG // QUAD)
            bt_g, cb_g, yoff_g = groups[g]
            gc0 = g * SSD_HPG * SSD_HEADDIM
            h0 = q * QUAD
            c0 = h0 * SSD_HEADDIM
            slot = ck * n_quads + q
            for j in range(QUAD):
                rhs_scr[slot, pl.ds(j * T, T), j * SSD_HEADDIM:(j + 1) * SSD_HEADDIM] = (
                    xbc_ref[0, rows, c0 + j * SSD_HEADDIM:c0 + (j + 1) * SSD_HEADDIM])
            decays = []
            ar_even = None
            for j in range(QUAD):
                hh = h0 + j
                lhs_scr[slot, boff:boff + SSD_STATE, j * T:(j + 1) * T] = (
                    bt_g * w_t[hh:hh + 1]).astype(BF16)
                ar = jnp.broadcast_to(cs[:, la_c0 + hh:la_c0 + hh + 1], (T, LANES))
                if with_y:
                    seg = jnp.where(keep, ar - adt_t[hh:hh + 1], NEG_BIG)
                    lhs_scr[slot, 0:T, j * T:(j + 1) * T] = (cb_g * jnp.exp2(seg)).astype(BF16)
                if j % 2 == 0:
                    ar_even = ar
                    continue
                ar_pair = jnp.where(lane_pair, ar_even, ar)
                pc0 = c0 + (j // 2) * LANES
                if with_y:
                    yo = yoff_g[:, pc0 - gc0:pc0 - gc0 + LANES]
                    y_put(pc0, yo * jnp.exp2(ar_pair))
                decays.append(jnp.exp2(ar_pair[end_row:end_row + 1]))
            return decays

        def finish(q, decays):
            slot = ck * n_quads + q
            c0 = q * QUAD * SSD_HEADDIM
            res = _dot(lhs_scr[slot], rhs_scr[slot])
            for p in range(QUAD // 2):
                pc0 = c0 + p * LANES
                if with_y:
                    y_pair = y_get(pc0) + res[:T, p * LANES:(p + 1) * LANES]
                    if fuse_out:
                        y_put(pc0, y_pair)
                    else:
                        y_o[0, rows, pc0:pc0 + LANES] = y_pair.astype(y_o.dtype)
                s_scr[:, pc0:pc0 + LANES] = (s_scr[:, pc0:pc0 + LANES] * decays[p]
                                             + res[boff:, p * LANES:(p + 1) * LANES])

        for q in range(n_quads):
            finish(q, prep(q))

        if fuse_out:
            xs = xbc_ref[0, rows, 0:D_SSD].astype(F32)
            y = y_scr[ck] + yf_ref[0, rows, :].astype(F32) + dskip_ref[...] * xs
            gy = y * z_ref[0, rows, :].astype(F32)
            ms = jnp.mean(gy * gy, axis=-1, keepdims=True)
            y_o[0, rows, :] = (gy * lax.rsqrt(ms + EPS) * ng_ref[...]).astype(y_o.dtype)

    order = list(range(cps - 1, -1, -1) if reverse else range(cps))
    constants = {ck: chunk_constants(ck) for ck in order}
    for ck in order:
        scan_chunk(ck, constants[ck])

    @pl.when(i == n_i - 1)
    def _():
        hfin_o[0] = s_scr[...]


def _ssd_call(xbc, dtla, h0, reverse, with_y, fuse=None, cps=SSD_CHUNKS_PER_STEP):
    bsz, n_rows, _ = xbc.shape
    T = SSD_CHUNK
    cps = min(cps, n_rows // T)
    rows = cps * T
    ns = n_rows // rows
    cmap = (lambda b, i: (b, ns - 1 - i, 0)) if reverse else (lambda b, i: (b, i, 0))
    bmap = lambda b, i: (b, 0, 0)
    in_specs = [pl.BlockSpec((1, rows, D_XBC), cmap), pl.BlockSpec((1, rows, LANES), cmap)]
    args = [xbc, dtla]
    if h0 is not None:
        in_specs.append(pl.BlockSpec((1, SSD_STATE, D_SSD), bmap))
        args.append(h0)
    if fuse is not None:
        yf, zg, dskip, ng = fuse
        in_specs += [pl.BlockSpec((1, rows, D_SSD), cmap), pl.BlockSpec((1, rows, D_SSD), cmap),
                     _resident((1, D_SSD)), _resident((1, D_SSD))]
        args += [yf, zg, dskip, ng]
    out_shape, out_specs = [], []
    if with_y:
        out_shape.append(jax.ShapeDtypeStruct((bsz, n_rows, D_SSD), BF16))
        out_specs.append(pl.BlockSpec((1, rows, D_SSD), cmap))
    out_shape.append(jax.ShapeDtypeStruct((bsz, SSD_STATE, D_SSD), F32))
    out_specs.append(pl.BlockSpec((1, SSD_STATE, D_SSD), bmap))
    scratch = [pltpu.VMEM((SSD_STATE, D_SSD), F32),
               pltpu.VMEM((cps * SSD_HEADS // QUAD, QUAD * T, QW), BF16),
               pltpu.VMEM((cps * SSD_HEADS // QUAD, (T if with_y else 0) + SSD_STATE, QUAD * T), BF16)]
    if with_y:
        scratch.append(pltpu.VMEM((cps, T, D_SSD), F32))
    name = "ssd_" + ("rev" if reverse else "fwd") + ("_y" if with_y else "_state") + ("_out" if fuse else "")
    return pl.pallas_call(
        functools.partial(_ssd_kernel, reverse=reverse, has_h0=h0 is not None, with_y=with_y,
                          fuse_out=fuse is not None, cps=cps),
        grid=(bsz, ns),
        in_specs=in_specs, out_specs=out_specs, out_shape=out_shape,
        scratch_shapes=scratch,
        compiler_params=pltpu.CompilerParams(dimension_semantics=("arbitrary", "arbitrary"),
                                             vmem_limit_bytes=_vmem_limit(52 << 20)),
        name=name,
    )(*args)


ATT_STRIP = 16
ATT_QB = 4
ATT_AHEAD = 1


def _attn_kernel(q_ref, kp_ref, kc_ref, kn_ref, vp_ref, vc_ref, vn_ref, kx_ref, vx_ref, sink_ref, o_ref,
                 s_scr, p_scr, bias_scr):
    blk = ATT_BLOCK
    n = pl.program_id(1)
    n_n = pl.num_programs(1)
    row = lax.broadcasted_iota(jnp.int32, (blk, blk), 0)
    col = lax.broadcasted_iota(jnp.int32, (blk, blk), 1)
    lane_kv = lax.broadcasted_iota(jnp.int32, (1, D_KV), 1) // ATT_HEADDIM
    kv_masks = [(lane_kv == j) for j in range(ATT_KV_HEADS)]
    kwin = [kp_ref[0]] + [kc_ref[0, sb * blk:(sb + 1) * blk, :] for sb in range(ATT_QB)] + [kn_ref[0]]
    vwin = [vp_ref[0]] + [vc_ref[0, sb * blk:(sb + 1) * blk, :] for sb in range(ATT_QB)] + [vn_ref[0]]
    kcats, vcats = [], []
    for sb in range(ATT_QB):
        prev_ok = (col >= row) if sb > 0 else jnp.logical_and(col >= row, n > 0)
        next_ok = (col <= row) if sb < ATT_QB - 1 else jnp.logical_and(col <= row, n < n_n - 1)
        bias_scr[2 * sb] = jnp.where(prev_ok, 0.0, NEG_BIG)
        bias_scr[2 * sb + 1] = jnp.where(next_ok, 0.0, NEG_BIG)
        kcats.append(jnp.concatenate(kwin[sb:sb + 3] + [kx_ref[0]], axis=0))
        vcats.append(jnp.concatenate(vwin[sb:sb + 3] + [vx_ref[0]], axis=0))
    nk = kcats[0].shape[0]
    pairs = [(sb, g) for sb in range(ATT_QB) for g in range(ATT_GROUP)]

    def scores(sb, g):
        qg = q_ref[0, sb * blk:(sb + 1) * blk, g * D_KV:(g + 1) * D_KV]
        qs = jnp.concatenate([jnp.where(kv_masks[j], qg, jnp.zeros_like(qg))
                              for j in range(ATT_KV_HEADS)], axis=0)
        s_scr[sb * ATT_GROUP + g] = _dot_nt(qs, kcats[sb])

    for sb, g in pairs[:ATT_AHEAD]:
        scores(sb, g)
    for idx, (sb, g) in enumerate(pairs):
        if idx + ATT_AHEAD < len(pairs):
            scores(*pairs[idx + ATT_AHEAD])
        slot = sb * ATT_GROUP + g
        vcat = vcats[sb]
        for st in range(ATT_KV_HEADS * blk // ATT_STRIP):
            r0 = st * ATT_STRIP
            i0 = r0 % blk
            sink = sink_ref[(r0 // blk) * ATT_GROUP + g] * LOG2E
            rows = pl.ds(r0, ATT_STRIP)
            parts = [s_scr[slot, rows, 0:blk] + bias_scr[2 * sb, pl.ds(i0, ATT_STRIP), :],
                     s_scr[slot, rows, blk:2 * blk],
                     s_scr[slot, rows, 2 * blk:3 * blk] + bias_scr[2 * sb + 1, pl.ds(i0, ATT_STRIP), :]]
            parts += [s_scr[slot, rows, c0:c0 + LANES] for c0 in range(3 * blk, nk, LANES)]
            mel = parts[0]
            for t in parts[1:]:
                mel = jnp.maximum(mel, t)
            m = jnp.maximum(jnp.max(mel, axis=-1, keepdims=True), sink)
            ps = [jnp.exp2(t - m) for t in parts]
            tot = ps[0]
            for t in ps[1:]:
                tot = tot + t
            den = jnp.sum(tot, axis=-1, keepdims=True) + jnp.exp2(sink - m)
            inv = 1.0 / den
            for ci, t in enumerate(ps):
                p_scr[slot, rows, ci * LANES:(ci + 1) * LANES] = (t * inv).astype(BF16)
        o = _dot(p_scr[slot], vcat)
        og = jnp.zeros((blk, D_KV), F32)
        for j in range(ATT_KV_HEADS):
            og = jnp.where(kv_masks[j], o[j * blk:(j + 1) * blk], og)
        o_ref[0, sb * blk:(sb + 1) * blk, g * D_KV:(g + 1) * D_KV] = og.astype(o_ref.dtype)


def _attn_call(q, k, v, kx, vx, sink):
    bsz, n_rows, _ = q.shape
    blk = ATT_BLOCK
    nb = n_rows // blk
    lc = kx.shape[1]
    cur = lambda b, n: (b, n, 0)
    prev = lambda b, n: (b, jnp.maximum(n * ATT_QB - 1, 0), 0)
    nxt = lambda b, n: (b, jnp.minimum((n + 1) * ATT_QB, nb - 1), 0)
    ctx = lambda b, n: (b, 0, 0)
    edge_spec = lambda m: pl.BlockSpec((1, blk, D_KV), m)
    cur_spec = pl.BlockSpec((1, ATT_QB * blk, D_KV), cur)
    n_slots = ATT_QB * ATT_GROUP
    return pl.pallas_call(
        _attn_kernel,
        grid=(bsz, nb // ATT_QB),
        in_specs=[pl.BlockSpec((1, ATT_QB * blk, D_Q), cur),
                  edge_spec(prev), cur_spec, edge_spec(nxt),
                  edge_spec(prev), cur_spec, edge_spec(nxt),
                  pl.BlockSpec((1, lc, D_KV), ctx), pl.BlockSpec((1, lc, D_KV), ctx),
                  pl.BlockSpec(memory_space=pltpu.SMEM)],
        out_specs=pl.BlockSpec((1, ATT_QB * blk, D_Q), cur),
        out_shape=jax.ShapeDtypeStruct((bsz, n_rows, D_Q), BF16),
        scratch_shapes=[pltpu.VMEM((n_slots, ATT_KV_HEADS * blk, 3 * blk + lc), F32),
                        pltpu.VMEM((n_slots, ATT_KV_HEADS * blk, 3 * blk + lc), BF16),
                        pltpu.VMEM((2 * ATT_QB, blk, blk), F32)],
        compiler_params=pltpu.CompilerParams(dimension_semantics=("arbitrary", "arbitrary"),
                                             vmem_limit_bytes=_vmem_limit(48 << 20)),
        name="attn",
    )(q, k, k, k, v, v, v, kx, vx, sink)


def _merge_kernel(y_ref, a_ref, gate_ref, x_ref, mod_ref, wos_ref, woa_ref, wout_ref, o_ref):
    gates = gate_ref[0].astype(F32)
    m = (gates[:, :D_MODEL] * _dot(y_ref[0], wos_ref[...])
         + gates[:, D_MODEL:] * _dot(a_ref[0], woa_ref[...]))
    o = _dot(m.astype(BF16), wout_ref[...])
    gt1 = mod_ref[0, :, 2 * D_MODEL:3 * D_MODEL]
    o_ref[0] = x_ref[0] + gt1 * o


def _merge_call(yn, att, gates, x, modv, w_os, w_oa, w_out, tm):
    bsz, n_rows, _ = x.shape
    row_map = lambda b, i: (b, i, 0)
    return pl.pallas_call(
        _merge_kernel,
        grid=(bsz, n_rows // tm),
        in_specs=[pl.BlockSpec((1, tm, D_SSD), row_map), pl.BlockSpec((1, tm, D_Q), row_map),
                  pl.BlockSpec((1, tm, 2 * D_MODEL), row_map), pl.BlockSpec((1, tm, D_MODEL), row_map),
                  pl.BlockSpec((1, 1, N_MOD * D_MODEL), lambda b, i: (b, 0, 0)),
                  _resident((D_SSD, D_MODEL)), _resident((D_Q, D_MODEL)), _resident((D_MODEL, D_MODEL))],
        out_specs=pl.BlockSpec((1, tm, D_MODEL), row_map),
        out_shape=jax.ShapeDtypeStruct((bsz, n_rows, D_MODEL), F32),
        compiler_params=pltpu.CompilerParams(dimension_semantics=("arbitrary", "arbitrary"),
                                             vmem_limit_bytes=_vmem_limit(48 << 20)),
        name="merge",
    )(yn, att, gates, x, modv, w_os, w_oa, w_out)


FF_CHUNK = 256


def _ffn_kernel(x_ref, xp_ref, xn_ref, mod_ref, g_ref, wup_ref, cw_ref, cb_ref, wdn_ref, fg_ref,
                o_ref, ua_scr, ub_scr, act_scr, *, tm):
    i = pl.program_id(1)
    n_i = pl.num_programs(1)
    g = g_ref[...]
    shift = mod_ref[0, :, 3 * D_MODEL:4 * D_MODEL]
    scale = mod_ref[0, :, 4 * D_MODEL:5 * D_MODEL]
    gt2 = mod_ref[0, :, 5 * D_MODEL:6 * D_MODEL]
    x = x_ref[0]
    h = _modulated_norm(x, g, shift, scale)
    hp = _modulated_norm(xp_ref[0], g, shift, scale) * (i > 0).astype(F32)
    hn = _modulated_norm(xn_ref[0], g, shift, scale) * (i < n_i - 1).astype(F32)
    h_ext = jnp.concatenate([h, hn, hp], axis=0).astype(BF16)
    for c in range(D_FF // FF_CHUNK):
        c0 = c * FF_CHUNK
        ua_scr[c] = _dot(h_ext, wup_ref[:, c0:c0 + FF_CHUNK])
        ub_scr[c] = _dot(h_ext, wup_ref[:, D_FF + c0:D_FF + c0 + FF_CHUNK])
    for c in range(D_FF // FF_CHUNK):
        c0 = c * FF_CHUNK
        b0 = D_FF + c0
        a = _conv3_rows(ua_scr[c], tm, cw_ref[:, c0:c0 + FF_CHUNK], cb_ref[:, c0:c0 + FF_CHUNK])
        b = _conv3_rows(ub_scr[c], tm, cw_ref[:, b0:b0 + FF_CHUNK], cb_ref[:, b0:b0 + FF_CHUNK])
        act_scr[:, c0:c0 + FF_CHUNK] = (_silu(a) * b).astype(BF16)
    f = _dot(act_scr[...], wdn_ref[...])
    x2 = x + gt2 * f
    ms = jnp.mean(x2 * x2, axis=-1, keepdims=True)
    o_ref[0] = x2 * lax.rsqrt(ms + EPS) * fg_ref[...]


def _ffn_call(x1, modv, norm_g, w_up, conv_w, conv_b, w_down, final_g, tm):
    bsz, n_rows, _ = x1.shape
    prev_map, next_map = _halo_maps(tm, n_rows)
    row_map = lambda b, i: (b, i, 0)
    return pl.pallas_call(
        functools.partial(_ffn_kernel, tm=tm),
        grid=(bsz, n_rows // tm),
        in_specs=[pl.BlockSpec((1, tm, D_MODEL), row_map),
                  pl.BlockSpec((1, HALO, D_MODEL), prev_map),
                  pl.BlockSpec((1, HALO, D_MODEL), next_map),
                  pl.BlockSpec((1, 1, N_MOD * D_MODEL), lambda b, i: (b, 0, 0)),
                  _resident((1, D_MODEL)),
                  _resident((D_MODEL, 2 * D_FF)), _resident((3, 2 * D_FF)), _resident((1, 2 * D_FF)),
                  _resident((D_FF, D_MODEL)), _resident((1, D_MODEL))],
        out_specs=pl.BlockSpec((1, tm, D_MODEL), row_map),
        out_shape=jax.ShapeDtypeStruct((bsz, n_rows, D_MODEL), F32),
        scratch_shapes=[pltpu.VMEM((D_FF // FF_CHUNK, tm + 2 * HALO, FF_CHUNK), F32),
                        pltpu.VMEM((D_FF // FF_CHUNK, tm + 2 * HALO, FF_CHUNK), F32),
                        pltpu.VMEM((tm, D_FF), BF16)],
        compiler_params=pltpu.CompilerParams(dimension_semantics=("arbitrary", "arbitrary"),
                                             vmem_limit_bytes=_vmem_limit(48 << 20)),
        name="ffn",
    )(x1, x1, x1, modv, norm_g, w_up, conv_w, conv_b, w_down, final_g)


def _rope_tables(n_rows):
    half = ATT_HEADDIM // 2
    inv = ROPE_BASE ** (-jnp.arange(0, half, 2, dtype=F32) / half)
    pos = jnp.arange(n_rows)
    rowp = (pos // GRID_W).astype(F32)[:, None] * inv[None]
    colp = (pos % GRID_W).astype(F32)[:, None] * inv[None]
    cos = jnp.concatenate([jnp.cos(rowp), jnp.cos(rowp), jnp.cos(colp), jnp.cos(colp)], axis=-1)
    sin = jnp.concatenate([-jnp.sin(rowp), jnp.sin(rowp), -jnp.sin(colp), jnp.sin(colp)], axis=-1)
    reps = LANES // ATT_HEADDIM
    return jnp.tile(cos, (1, reps)), jnp.tile(sin, (1, reps))


def kernel(x, c, ctx, c_ctx, w_mod, b_mod, norm1_g, norm2_g, w_in, ssd_conv_w, ssd_conv_b, ssd_dt_bias,
           ssd_a_log, ssd_d, ssd_norm_g, w_o_ssd, w_o_att, att_sink, w_out, w_up, ffn_conv_w, ffn_conv_b,
           w_down, final_g):
    depth = w_mod.shape[0]
    assert depth == 1, "single-layer block"
    l = 0
    bsz, n_lat, _ = x.shape
    assert bsz + 1 <= SUBLANES

    cc = jnp.concatenate([c, c_ctx[None], jnp.zeros((SUBLANES - bsz - 1, D_MODEL), F32)], axis=0)
    mod_all = _mod_call(cc, w_mod[l], b_mod[l][None])
    mod_lat = mod_all[:bsz, None, :]
    mod_ctx = mod_all[bsz:bsz + 1, None, :]

    wi = w_in[l].astype(BF16)
    w_q = (wi[:, Q0:K0].reshape(D_MODEL, ATT_KV_HEADS, ATT_GROUP, ATT_HEADDIM)
           .transpose(0, 2, 1, 3).reshape(D_MODEL, D_Q))
    w_qkvg = jnp.concatenate([w_q, wi[:, K0:D_IN]], axis=1)
    dtb = ssd_dt_bias[l].reshape(1, 2 * SSD_HEADS)
    a_mult = -jnp.exp(ssd_a_log[l].reshape(1, 2 * SSD_HEADS))
    consts = {"conv_w": ssd_conv_w[l], "conv_b": ssd_conv_b[l][None],
              "dt_bias": jnp.concatenate([dtb, dtb], axis=1),
              "a_mult": jnp.concatenate([jnp.ones_like(a_mult), a_mult], axis=1)}
    rope = _rope_tables(n_lat)
    g1 = norm1_g[l][None]

    xbc_c, dtla_c, k_c, v_c = _inproj_call(ctx, mod_ctx, False, g1, wi, w_qkvg, consts, None,
                                           ctx.shape[1], False)
    xbc_l, dtla_l, k_l, v_l, zg, q_l, gates = _inproj_call(x, mod_lat, True, g1, wi, w_qkvg, consts, rope,
                                                           INPROJ_ROWS, True)

    (hf_c,) = _ssd_call(xbc_c, dtla_c, None, False, False)
    (hb_c,) = _ssd_call(xbc_c, dtla_c, None, True, False)
    yf, _ = _ssd_call(xbc_l, dtla_l, hf_c, False, True)
    dskip = jnp.repeat(ssd_d[l], SSD_HEADDIM)[None]
    yn, _ = _ssd_call(xbc_l, dtla_l, hb_c, True, True, fuse=(yf, zg, dskip, ssd_norm_g[l][None]))

    sink = att_sink[l].astype(F32)
    att = _attn_call(q_l, k_l, v_l, k_c, v_c, sink)

    w_oa = (w_o_att[l].reshape(ATT_KV_HEADS, ATT_GROUP, ATT_HEADDIM, D_MODEL)
            .transpose(1, 0, 2, 3).reshape(D_Q, D_MODEL))
    x1 = _merge_call(yn, att, gates, x, mod_lat, w_o_ssd[l].astype(BF16),
                     w_oa.astype(BF16), w_out[l].astype(BF16), MERGE_ROWS)
    return _ffn_call(x1, mod_lat, norm2_g[l][None], w_up[l].astype(BF16), ffn_conv_w[l],
                     ffn_conv_b[l][None], w_down[l].astype(BF16), final_g[None], FFN_ROWS)
```

```python
import functools

import jax
import jax.numpy as jnp
import numpy as np
from jax import lax
from jax.experimental import pallas as pl
from jax.experimental.pallas import tpu as pltpu

F32 = jnp.float32
BF16 = jnp.bfloat16

D_MODEL = 1024
GRID_W = 64
EPS = 1e-6
N_MOD = 6
D_SSD = 2048
SSD_HEADDIM = 64
SSD_HEADS = 32
SSD_GROUPS = 4
SSD_HPG = 8
SSD_STATE = 128
SSD_CHUNK = 128
D_BC = SSD_GROUPS * SSD_STATE
D_XBC = D_SSD + 2 * D_BC
ATT_HEADS = 16
ATT_KV_HEADS = 4
ATT_GROUP = 4
ATT_HEADDIM = 64
ATT_BLOCK = 128
D_Q = ATT_HEADS * ATT_HEADDIM
D_KV = ATT_KV_HEADS * ATT_HEADDIM
ROPE_BASE = 10000.0
D_FF = 2816
Z0 = 0
XBC0 = Z0 + D_SSD
DT0 = XBC0 + D_XBC
Q0 = DT0 + 2 * SSD_HEADS
K0 = Q0 + D_Q
V0 = K0 + D_KV
G0 = V0 + D_KV
D_IN = G0 + 2 * D_MODEL

V7X_VMEM_BYTES = 64 * 1024 * 1024
LANES = 128
SUBLANES = 8
HALO = SUBLANES
NEG_BIG = -1e30
LOG2E = 1.4426950408889634

INPROJ_ROWS = 256
MERGE_ROWS = 512
FFN_ROWS = 512
SSD_CHUNKS_PER_STEP = 4


def _vmem_limit(nbytes):
    return int(min(nbytes, V7X_VMEM_BYTES - 8 * 1024 * 1024))


def _resident(shape, index=None):
    index = (0,) * len(shape) if index is None else tuple(index)
    return pl.BlockSpec(shape, lambda *_: index, pipeline_mode=pl.Buffered(1))


def _sigmoid(x):
    return 1.0 / (1.0 + jnp.exp2(x * (-LOG2E)))


def _silu(x):
    return x * _sigmoid(x)


def _dot(a, b):
    return jnp.dot(a, b, preferred_element_type=F32)


def _dot_nt(a, b):
    return lax.dot_general(a, b, (((1,), (1,)), ((), ())), preferred_element_type=F32)


def _modulated_norm(x, g, shift, scale):
    ms = jnp.mean(x * x, axis=-1, keepdims=True)
    return x * lax.rsqrt(ms + EPS) * g * (1.0 + scale) + shift


def _mod_kernel(c_ref, w_ref, b_ref, o_ref):
    s = _silu(c_ref[...])
    o_ref[...] = _dot(s.astype(BF16), w_ref[...].astype(BF16)) + b_ref[...]


def _mod_call(cc, w_mod, b_mod):
    n = w_mod.shape[1]
    tn = 1536
    return pl.pallas_call(
        _mod_kernel,
        grid=(n // tn,),
        in_specs=[pl.BlockSpec((SUBLANES, D_MODEL), lambda j: (0, 0)),
                  pl.BlockSpec((D_MODEL, tn), lambda j: (0, j)),
                  pl.BlockSpec((1, tn), lambda j: (0, j))],
        out_specs=pl.BlockSpec((SUBLANES, tn), lambda j: (0, j)),
        out_shape=jax.ShapeDtypeStruct((SUBLANES, n), F32),
        compiler_params=pltpu.CompilerParams(dimension_semantics=("arbitrary",),
                                             vmem_limit_bytes=_vmem_limit(40 << 20)),
        name="mod",
    )(cc, w_mod, b_mod)


def _halo_maps(tm, n_rows):
    hb = tm // HALO
    last = n_rows // HALO - 1
    prev_map = lambda b, i: (b, jnp.maximum(i * hb - 1, 0), 0)
    next_map = lambda b, i: (b, jnp.minimum((i + 1) * hb, last), 0)
    return prev_map, next_map


def _conv3_rows(u, tm, w, b):
    n = u.shape[0]
    before = pltpu.roll(u, 1, 0)[0:tm]
    after = pltpu.roll(u, n - 1, 0)[0:tm]
    return before * w[0:1] + u[0:tm] * w[1:2] + after * w[2:3] + b


def _store_normed_rows(h_scr, x_ref, xp_ref, xn_ref, g, shift, scale, tm):
    i = pl.program_id(1)
    n_i = pl.num_programs(1)
    h = _modulated_norm(x_ref[0], g, shift, scale)
    hp = _modulated_norm(xp_ref[0], g, shift, scale) * (i > 0).astype(F32)
    hn = _modulated_norm(xn_ref[0], g, shift, scale) * (i < n_i - 1).astype(F32)
    h_scr[0:tm, :] = h.astype(BF16)
    h_scr[tm:tm + 2 * HALO, :] = jnp.concatenate([hn, hp], axis=0).astype(BF16)


def _rope(t, cos, sin_signed, n_tiles):
    width = t.shape[1]
    lane = lax.broadcasted_iota(jnp.int32, t.shape, 1)
    first = (lane % 32) < 16
    partner = jnp.where(first, pltpu.roll(t, width - 16, 1), pltpu.roll(t, 16, 1))
    cos_t = jnp.concatenate([cos] * n_tiles, axis=1)
    sin_t = jnp.concatenate([sin_signed] * n_tiles, axis=1)
    return t * cos_t + partner * sin_t


XBC_WINDOWS = (512, 2560)
assert sum(XBC_WINDOWS) == D_XBC
QKVG_K0 = D_Q
QKVG_V0 = QKVG_K0 + D_KV
QKVG_G0 = QKVG_V0 + D_KV
QKVG_END = QKVG_G0 + 2 * D_MODEL
P_DT0 = 0
P_B0 = LANES
P_Z0 = P_B0 + QKVG_END


def _inproj_kernel(*refs, tm, latent):
    n_win = len(XBC_WINDOWS)
    x_ref, xp_ref, xn_ref, mod_ref, g_ref = refs[:5]
    wx_refs = refs[5:5 + n_win]
    rest = refs[5 + n_win:]
    if latent:
        (wdt_ref, wb_ref, wz_ref, cw_ref, cb_ref, dtb_ref, am_ref, cos_ref, sin_ref,
         xbc_o, dtla_o, k_o, v_o, z_o, q_o, gate_o, u_scr, p_scr, h_scr) = rest
        pk0, pv0 = P_B0 + QKVG_K0, P_B0 + QKVG_V0
    else:
        (wdt_ref, wb_ref, cw_ref, cb_ref, dtb_ref, am_ref,
         xbc_o, dtla_o, k_o, v_o, u_scr, p_scr, h_scr) = rest
        pk0, pv0 = P_B0, P_B0 + D_KV
    i = pl.program_id(1)

    @pl.when(i >= 0)
    def _():
        _store_normed_rows(h_scr, x_ref, xp_ref, xn_ref, g_ref[...], mod_ref[0, :, 0:D_MODEL],
                           mod_ref[0, :, D_MODEL:2 * D_MODEL], tm)

    c0 = 0
    for wx_ref in wx_refs:
        u_scr[:, c0:c0 + wx_ref.shape[1]] = _dot(h_scr[...], wx_ref[...])
        c0 += wx_ref.shape[1]
    p_scr[:, P_DT0:P_B0] = _dot(h_scr[0:tm, :], wdt_ref[...])
    p_scr[:, P_B0:P_B0 + wb_ref.shape[1]] = _dot(h_scr[0:tm, :], wb_ref[...])
    if latent:
        p_scr[:, P_Z0:P_Z0 + D_SSD] = _dot(h_scr[0:tm, :], wz_ref[...])

    cchunk = 512
    for c0 in range(0, D_XBC, cchunk):
        cols = slice(c0, c0 + cchunk)
        n = tm + 2 * HALO
        before = jnp.concatenate([u_scr[n - 1:n, cols], u_scr[0:tm - 1, cols]], axis=0)
        after = jnp.concatenate([u_scr[1:tm, cols], u_scr[tm:tm + 1, cols]], axis=0)
        y = (before * cw_ref[0:1, cols] + u_scr[0:tm, cols] * cw_ref[1:2, cols]
             + after * cw_ref[2:3, cols] + cb_ref[:, cols])
        xbc_o[0, :, cols] = _silu(y).astype(BF16)

    raw = p_scr[:, P_DT0:P_B0]
    lane = lax.broadcasted_iota(jnp.int32, raw.shape, 1)
    raw = jnp.where(lane < 2 * SSD_HEADS, raw, pltpu.roll(raw, 2 * SSD_HEADS, 1)) + dtb_ref[...]
    dt = jnp.maximum(raw, 0.0) + jnp.log1p(jnp.exp(-jnp.abs(raw)))
    dtla_o[0] = jnp.where(lane < 2 * SSD_HEADS, dt, dt * am_ref[...])

    kf = p_scr[:, pk0:pk0 + D_KV]
    if latent:
        cos = cos_ref[...]
        sin = sin_ref[...]
        kf = _rope(kf, cos, sin, D_KV // LANES)
    k_o[0] = kf.astype(BF16)
    v_o[0] = p_scr[:, pv0:pv0 + D_KV].astype(BF16)
    if latent:
        z_o[0] = _silu(p_scr[:, P_Z0:P_Z0 + D_SSD]).astype(BF16)
        qf = _rope(p_scr[:, P_B0:P_B0 + D_Q], cos, sin, D_Q // LANES)
        q_o[0] = (qf * (ATT_HEADDIM ** -0.5 * LOG2E)).astype(BF16)
        gate_o[0] = _sigmoid(p_scr[:, P_B0 + QKVG_G0:P_B0 + QKVG_END]).astype(BF16)


def _inproj_call(x, modv, per_batch_mod, norm_g, w_in_bf16, w_qkvg, consts, rope, tm, latent):
    bsz, n_rows, _ = x.shape
    prev_map, next_map = _halo_maps(tm, n_rows)
    row_map = lambda b, i: (b, i, 0)
    mod_map = (lambda b, i: (b, 0, 0)) if per_batch_mod else (lambda b, i: (0, 0, 0))
    assert DT0 % LANES == 0
    xbc_specs, start = [], XBC0
    for width in XBC_WINDOWS:
        assert start % width == 0
        xbc_specs.append(_resident((D_MODEL, width), (0, start // width)))
        start += width
    n_win = len(XBC_WINDOWS)
    in_specs = ([pl.BlockSpec((1, tm, D_MODEL), row_map),
                 pl.BlockSpec((1, HALO, D_MODEL), prev_map),
                 pl.BlockSpec((1, HALO, D_MODEL), next_map),
                 pl.BlockSpec((1, 1, N_MOD * D_MODEL), mod_map),
                 _resident((1, D_MODEL))]
                + xbc_specs
                + [_resident((D_MODEL, LANES), (0, DT0 // LANES))])
    args = [x, x, x, modv, norm_g] + [w_in_bf16] * (n_win + 1)
    if latent:
        in_specs += [_resident((D_MODEL, QKVG_END)), _resident((D_MODEL, D_SSD), (0, 0))]
        args += [w_qkvg, w_in_bf16]
    else:
        assert QKVG_K0 % (2 * D_KV) == 0
        in_specs += [_resident((D_MODEL, 2 * D_KV), (0, QKVG_K0 // (2 * D_KV)))]
        args += [w_qkvg]
    in_specs += [_resident((3, D_XBC)), _resident((1, D_XBC)), _resident((1, LANES)), _resident((1, LANES))]
    args += [consts["conv_w"], consts["conv_b"], consts["dt_bias"], consts["a_mult"]]
    out_shape = [jax.ShapeDtypeStruct((bsz, n_rows, D_XBC), BF16),
                 jax.ShapeDtypeStruct((bsz, n_rows, LANES), F32),
                 jax.ShapeDtypeStruct((bsz, n_rows, D_KV), BF16),
                 jax.ShapeDtypeStruct((bsz, n_rows, D_KV), BF16)]
    out_specs = [pl.BlockSpec((1, tm, D_XBC), row_map), pl.BlockSpec((1, tm, LANES), row_map),
                 pl.BlockSpec((1, tm, D_KV), row_map), pl.BlockSpec((1, tm, D_KV), row_map)]
    if latent:
        in_specs += [pl.BlockSpec((tm, LANES), lambda b, i: (i, 0)),
                     pl.BlockSpec((tm, LANES), lambda b, i: (i, 0))]
        args += [rope[0], rope[1]]
        out_shape += [jax.ShapeDtypeStruct((bsz, n_rows, D_SSD), BF16),
                      jax.ShapeDtypeStruct((bsz, n_rows, D_Q), BF16),
                      jax.ShapeDtypeStruct((bsz, n_rows, 2 * D_MODEL), BF16)]
        out_specs += [pl.BlockSpec((1, tm, D_SSD), row_map), pl.BlockSpec((1, tm, D_Q), row_map),
                      pl.BlockSpec((1, tm, 2 * D_MODEL), row_map)]
    return pl.pallas_call(
        functools.partial(_inproj_kernel, tm=tm, latent=latent),
        grid=(bsz, n_rows // tm),
        in_specs=in_specs, out_specs=out_specs, out_shape=out_shape,
        scratch_shapes=[pltpu.VMEM((tm + 2 * HALO, D_XBC), F32),
                        pltpu.VMEM((tm, (P_Z0 + D_SSD) if latent else (P_B0 + 2 * D_KV)), F32),
                        pltpu.VMEM((tm + 2 * HALO, D_MODEL), BF16)],
        compiler_params=pltpu.CompilerParams(dimension_semantics=("arbitrary", "arbitrary"),
                                             vmem_limit_bytes=_vmem_limit(56 << 20)),
        name="inproj_lat" if latent else "inproj_ctx",
    )(*args)


QUAD = 4
QW = QUAD * SSD_HEADDIM


def _ssd_kernel(*refs, reverse, has_h0, with_y, fuse_out, cps):
    T = SSD_CHUNK
    it = iter(refs)
    xbc_ref = next(it)
    dtla_ref = next(it)
    h0_ref = next(it) if has_h0 else None
    if fuse_out:
        yf_ref, z_ref, dskip_ref, ng_ref = next(it), next(it), next(it), next(it)
    y_o = next(it) if with_y else None
    hfin_o = next(it)
    s_scr = next(it)
    rhs_scr = next(it)
    lhs_scr = next(it)
    y_scr = next(it) if with_y else None

    i = pl.program_id(1)
    n_i = pl.num_programs(1)

    @pl.when(i == 0)
    def _():
        if has_h0:
            s_scr[...] = h0_ref[0]
        else:
            s_scr[...] = jnp.zeros_like(s_scr)
        rhs_scr[...] = jnp.zeros_like(rhs_scr)

    d = 1 if reverse else 0
    dt_c0 = d * SSD_HEADS
    la_c0 = 2 * SSD_HEADS + d * SSD_HEADS
    lane = lax.broadcasted_iota(jnp.int32, (T, LANES), 1)
    row = lax.broadcasted_iota(jnp.int32, (T, T), 0)
    col = lax.broadcasted_iota(jnp.int32, (T, T), 1)
    keep = (col >= row) if reverse else (col <= row)
    tri = jnp.where(keep, 1.0, 0.0).astype(BF16)
    end_row = 0 if reverse else T - 1
    lane_pair = lax.broadcasted_iota(jnp.int32, (1, LANES), 1) < SSD_HEADDIM
    n_quads = SSD_HEADS // QUAD

    def chunk_constants(ck):
        rows = pl.ds(ck * T, T)
        dtla = dtla_ref[0, rows, :]
        dtla2 = jnp.where(lane >= 2 * SSD_HEADS, dtla * LOG2E, dtla)
        hi = dtla2.astype(BF16)
        r1 = dtla2 - hi.astype(F32)
        mid = r1.astype(BF16)
        lo = (r1 - mid.astype(F32)).astype(BF16)
        cs3 = _dot(tri, jnp.concatenate([hi, mid, lo], axis=1))
        cs = cs3[:, 0:LANES] + cs3[:, LANES:2 * LANES] + cs3[:, 2 * LANES:3 * LANES]
        cs_t = cs.T
        dtla_t = dtla.T
        a_t = cs_t[la_c0:la_c0 + SSD_HEADS]
        a_end_t = cs_t[la_c0:la_c0 + SSD_HEADS, end_row:end_row + 1]
        dt_t = dtla_t[dt_c0:dt_c0 + SSD_HEADS]
        w_t = dt_t * jnp.exp2(a_end_t - a_t)
        adt_t = a_t - jnp.log2(dt_t)
        groups = []
        for g in range(SSD_GROUPS):
            b_g = xbc_ref[0, rows, D_SSD + g * SSD_STATE:D_SSD + (g + 1) * SSD_STATE]
            bt_g = b_g.astype(F32).T
            c_g = cb_g = None
            if with_y:
                c_g = xbc_ref[0, rows, D_SSD + D_BC + g * SSD_STATE:D_SSD + D_BC + (g + 1) * SSD_STATE]
                cb_g = _dot_nt(c_g, b_g)
            groups.append((bt_g, c_g, cb_g))
        return cs, w_t, adt_t, groups

    def scan_chunk(ck, constants):
        rows = pl.ds(ck * T, T)
        cs, w_t, adt_t, group_consts = constants

        def y_put(pc0, val):
            y_scr[ck, :, pc0:pc0 + LANES] = val

        def y_get(pc0):
            return y_scr[ck, :, pc0:pc0 + LANES]

        boff = T if with_y else 0
        groups = []
        for g, (bt_g, c_g, cb_g) in enumerate(group_consts):
            yoff_g = None
            if with_y:
                gc0 = g * SSD_HPG * SSD_HEADDIM
                s_g = s_scr[:, gc0:gc0 + SSD_HPG * SSD_HEADDIM].astype(BF16)
                yoff_g = _dot(c_g, s_g)
            groups.append((bt_g, cb_g, yoff_g))

        def prep(q):
            g = q // (SSD_HPG // QUAD)
            bt_g, cb_g, yoff_g = groups[g]
            gc0 = g * SSD_HPG * SSD_HEADDIM
            h0 = q * QUAD
            c0 = h0 * SSD_HEADDIM
            slot = ck * n_quads + q
            for j in range(QUAD):
                rhs_scr[slot, pl.ds(j * T, T), j * SSD_HEADDIM:(j + 1) * SSD_HEADDIM] = (
                    xbc_ref[0, rows, c0 + j * SSD_HEADDIM:c0 + (j + 1) * SSD_HEADDIM])
            decays = []
            ar_even = None
            for j in range(QUAD):
                hh = h0 + j
                lhs_scr[slot, boff:boff + SSD_STATE, j * T:(j + 1) * T] = (
                    bt_g * w_t[hh:hh + 1]).astype(BF16)
                ar = jnp.broadcast_to(cs[:, la_c0 + hh:la_c0 + hh + 1], (T, LANES))
                if with_y:
                    seg = jnp.where(keep, ar - adt_t[hh:hh + 1], NEG_BIG)
                    lhs_scr[slot, 0:T, j * T:(j + 1) * T] = (cb_g * jnp.exp2(seg)).astype(BF16)
                if j % 2 == 0:
                    ar_even = ar
                    continue
                ar_pair = jnp.where(lane_pair, ar_even, ar)
                pc0 = c0 + (j // 2) * LANES
                if with_y:
                    yo = yoff_g[:, pc0 - gc0:pc0 - gc0 + LANES]
                    y_put(pc0, yo * jnp.exp2(ar_pair))
                decays.append(jnp.exp2(ar_pair[end_row:end_row + 1]))
            return decays

        def finish(q, decays):
            slot = ck * n_quads + q
            c0 = q * QUAD * SSD_HEADDIM
            res = _dot(lhs_scr[slot], rhs_scr[slot])
            for p in range(QUAD // 2):
                pc0 = c0 + p * LANES
                if with_y:
                    y_pair = y_get(pc0) + res[:T, p * LANES:(p + 1) * LANES]
                    if fuse_out:
                        y_put(pc0, y_pair)
                    else:
                        y_o[0, rows, pc0:pc0 + LANES] = y_pair.astype(y_o.dtype)
                s_scr[:, pc0:pc0 + LANES] = (s_scr[:, pc0:pc0 + LANES] * decays[p]
                                             + res[boff:, p * LANES:(p + 1) * LANES])

        for q in range(n_quads):
            finish(q, prep(q))

        if fuse_out:
            xs = xbc_ref[0, rows, 0:D_SSD].astype(F32)
            y = y_scr[ck] + yf_ref[0, rows, :].astype(F32) + dskip_ref[...] * xs
            gy = y * z_ref[0, rows, :].astype(F32)
            ms = jnp.mean(gy * gy, axis=-1, keepdims=True)
            y_o[0, rows, :] = (gy * lax.rsqrt(ms + EPS) * ng_ref[...]).astype(y_o.dtype)

    order = list(range(cps - 1, -1, -1) if reverse else range(cps))
    constants = {ck: chunk_constants(ck) for ck in order}
    for ck in order:
        scan_chunk(ck, constants[ck])

    @pl.when(i == n_i - 1)
    def _():
        hfin_o[0] = s_scr[...]


def _ssd_call(xbc, dtla, h0, reverse, with_y, fuse=None, cps=SSD_CHUNKS_PER_STEP):
    bsz, n_rows, _ = xbc.shape
    T = SSD_CHUNK
    cps = min(cps, n_rows // T)
    rows = cps * T
    ns = n_rows // rows
    cmap = (lambda b, i: (b, ns - 1 - i, 0)) if reverse else (lambda b, i: (b, i, 0))
    bmap = lambda b, i: (b, 0, 0)
    in_specs = [pl.BlockSpec((1, rows, D_XBC), cmap), pl.BlockSpec((1, rows, LANES), cmap)]
    args = [xbc, dtla]
    if h0 is not None:
        in_specs.append(pl.BlockSpec((1, SSD_STATE, D_SSD), bmap))
        args.append(h0)
    if fuse is not None:
        yf, zg, dskip, ng = fuse
        in_specs += [pl.BlockSpec((1, rows, D_SSD), cmap), pl.BlockSpec((1, rows, D_SSD), cmap),
                     _resident((1, D_SSD)), _resident((1, D_SSD))]
        args += [yf, zg, dskip, ng]
    out_shape, out_specs = [], []
    if with_y:
        out_shape.append(jax.ShapeDtypeStruct((bsz, n_rows, D_SSD), BF16))
        out_specs.append(pl.BlockSpec((1, rows, D_SSD), cmap))
    out_shape.append(jax.ShapeDtypeStruct((bsz, SSD_STATE, D_SSD), F32))
    out_specs.append(pl.BlockSpec((1, SSD_STATE, D_SSD), bmap))
    scratch = [pltpu.VMEM((SSD_STATE, D_SSD), F32),
               pltpu.VMEM((cps * SSD_HEADS // QUAD, QUAD * T, QW), BF16),
               pltpu.VMEM((cps * SSD_HEADS // QUAD, (T if with_y else 0) + SSD_STATE, QUAD * T), BF16)]
    if with_y:
        scratch.append(pltpu.VMEM((cps, T, D_SSD), F32))
    name = "ssd_" + ("rev" if reverse else "fwd") + ("_y" if with_y else "_state") + ("_out" if fuse else "")
    return pl.pallas_call(
        functools.partial(_ssd_kernel, reverse=reverse, has_h0=h0 is not None, with_y=with_y,
                          fuse_out=fuse is not None, cps=cps),
        grid=(bsz, ns),
        in_specs=in_specs, out_specs=out_specs, out_shape=out_shape,
        scratch_shapes=scratch,
        compiler_params=pltpu.CompilerParams(dimension_semantics=("arbitrary", "arbitrary"),
                                             vmem_limit_bytes=_vmem_limit(52 << 20)),
        name=name,
    )(*args)


ATT_STRIP = 16
ATT_QB = 4
ATT_AHEAD = 1


def _attn_kernel(q_ref, kp_ref, kc_ref, kn_ref, vp_ref, vc_ref, vn_ref, kx_ref, vx_ref, sink_ref, o_ref,
                 s_scr, p_scr, bias_scr):
    blk = ATT_BLOCK
    n = pl.program_id(1)
    n_n = pl.num_programs(1)
    row = lax.broadcasted_iota(jnp.int32, (blk, blk), 0)
    col = lax.broadcasted_iota(jnp.int32, (blk, blk), 1)
    lane_kv = lax.broadcasted_iota(jnp.int32, (1, D_KV), 1) // ATT_HEADDIM
    kv_masks = [(lane_kv == j) for j in range(ATT_KV_HEADS)]
    kwin = [kp_ref[0]] + [kc_ref[0, sb * blk:(sb + 1) * blk, :] for sb in range(ATT_QB)] + [kn_ref[0]]
    vwin = [vp_ref[0]] + [vc_ref[0, sb * blk:(sb + 1) * blk, :] for sb in range(ATT_QB)] + [vn_ref[0]]
    kcats, vcats = [], []
    for sb in range(ATT_QB):
        prev_ok = (col >= row) if sb > 0 else jnp.logical_and(col >= row, n > 0)
        next_ok = (col <= row) if sb < ATT_QB - 1 else jnp.logical_and(col <= row, n < n_n - 1)
        bias_scr[2 * sb] = jnp.where(prev_ok, 0.0, NEG_BIG)
        bias_scr[2 * sb + 1] = jnp.where(next_ok, 0.0, NEG_BIG)
        kcats.append(jnp.concatenate(kwin[sb:sb + 3] + [kx_ref[0]], axis=0))
        vcats.append(jnp.concatenate(vwin[sb:sb + 3] + [vx_ref[0]], axis=0))
    nk = kcats[0].shape[0]
    pairs = [(sb, g) for sb in range(ATT_QB) for g in range(ATT_GROUP)]

    def scores(sb, g):
        qg = q_ref[0, sb * blk:(sb + 1) * blk, g * D_KV:(g + 1) * D_KV]
        qs = jnp.concatenate([jnp.where(kv_masks[j], qg, jnp.zeros_like(qg))
                              for j in range(ATT_KV_HEADS)], axis=0)
        s_scr[sb * ATT_GROUP + g] = _dot_nt(qs, kcats[sb])

    for sb, g in pairs[:ATT_AHEAD]:
        scores(sb, g)
    for idx, (sb, g) in enumerate(pairs):
        if idx + ATT_AHEAD < len(pairs):
            scores(*pairs[idx + ATT_AHEAD])
        slot = sb * ATT_GROUP + g
        vcat = vcats[sb]
        for st in range(ATT_KV_HEADS * blk // ATT_STRIP):
            r0 = st * ATT_STRIP
            i0 = r0 % blk
            sink = sink_ref[(r0 // blk) * ATT_GROUP + g] * LOG2E
            rows = pl.ds(r0, ATT_STRIP)
            parts = [s_scr[slot, rows, 0:blk] + bias_scr[2 * sb, pl.ds(i0, ATT_STRIP), :],
                     s_scr[slot, rows, blk:2 * blk],
                     s_scr[slot, rows, 2 * blk:3 * blk] + bias_scr[2 * sb + 1, pl.ds(i0, ATT_STRIP), :]]
            parts += [s_scr[slot, rows, c0:c0 + LANES] for c0 in range(3 * blk, nk, LANES)]
            mel = parts[0]
            for t in parts[1:]:
                mel = jnp.maximum(mel, t)
            m = jnp.maximum(jnp.max(mel, axis=-1, keepdims=True), sink)
            ps = [jnp.exp2(t - m) for t in parts]
            tot = ps[0]
            for t in ps[1:]:
                tot = tot + t
            den = jnp.sum(tot, axis=-1, keepdims=True) + jnp.exp2(sink - m)
            inv = 1.0 / den
            for ci, t in enumerate(ps):
                p_scr[slot, rows, ci * LANES:(ci + 1) * LANES] = (t * inv).astype(BF16)
        o = _dot(p_scr[slot], vcat)
        og = jnp.zeros((blk, D_KV), F32)
        for j in range(ATT_KV_HEADS):
            og = jnp.where(kv_masks[j], o[j * blk:(j + 1) * blk], og)
        o_ref[0, sb * blk:(sb + 1) * blk, g * D_KV:(g + 1) * D_KV] = og.astype(o_ref.dtype)


def _attn_call(q, k, v, kx, vx, sink):
    bsz, n_rows, _ = q.shape
    blk = ATT_BLOCK
    nb = n_rows // blk
    lc = kx.shape[1]
    cur = lambda b, n: (b, n, 0)
    prev = lambda b, n: (b, jnp.maximum(n * ATT_QB - 1, 0), 0)
    nxt = lambda b, n: (b, jnp.minimum((n + 1) * ATT_QB, nb - 1), 0)
    ctx = lambda b, n: (b, 0, 0)
    edge_spec = lambda m: pl.BlockSpec((1, blk, D_KV), m)
    cur_spec = pl.BlockSpec((1, ATT_QB * blk, D_KV), cur)
    n_slots = ATT_QB * ATT_GROUP
    return pl.pallas_call(
        _attn_kernel,
        grid=(bsz, nb // ATT_QB),
        in_specs=[pl.BlockSpec((1, ATT_QB * blk, D_Q), cur),
                  edge_spec(prev), cur_spec, edge_spec(nxt),
                  edge_spec(prev), cur_spec, edge_spec(nxt),
                  pl.BlockSpec((1, lc, D_KV), ctx), pl.BlockSpec((1, lc, D_KV), ctx),
                  pl.BlockSpec(memory_space=pltpu.SMEM)],
        out_specs=pl.BlockSpec((1, ATT_QB * blk, D_Q), cur),
        out_shape=jax.ShapeDtypeStruct((bsz, n_rows, D_Q), BF16),
        scratch_shapes=[pltpu.VMEM((n_slots, ATT_KV_HEADS * blk, 3 * blk + lc), F32),
                        pltpu.VMEM((n_slots, ATT_KV_HEADS * blk, 3 * blk + lc), BF16),
                        pltpu.VMEM((2 * ATT_QB, blk, blk), F32)],
        compiler_params=pltpu.CompilerParams(dimension_semantics=("arbitrary", "arbitrary"),
                                             vmem_limit_bytes=_vmem_limit(48 << 20)),
        name="attn",
    )(q, k, k, k, v, v, v, kx, vx, sink)


def _merge_kernel(y_ref, a_ref, gate_ref, x_ref, mod_ref, wos_ref, woa_ref, wout_ref, o_ref):
    gates = gate_ref[0].astype(F32)
    m = (gates[:, :D_MODEL] * _dot(y_ref[0], wos_ref[...])
         + gates[:, D_MODEL:] * _dot(a_ref[0], woa_ref[...]))
    o = _dot(m.astype(BF16), wout_ref[...])
    gt1 = mod_ref[0, :, 2 * D_MODEL:3 * D_MODEL]
    o_ref[0] = x_ref[0] + gt1 * o


def _merge_call(yn, att, gates, x, modv, w_os, w_oa, w_out, tm):
    bsz, n_rows, _ = x.shape
    row_map = lambda b, i: (b, i, 0)
    return pl.pallas_call(
        _merge_kernel,
        grid=(bsz, n_rows // tm),
        in_specs=[pl.BlockSpec((1, tm, D_SSD), row_map), pl.BlockSpec((1, tm, D_Q), row_map),
                  pl.BlockSpec((1, tm, 2 * D_MODEL), row_map), pl.BlockSpec((1, tm, D_MODEL), row_map),
                  pl.BlockSpec((1, 1, N_MOD * D_MODEL), lambda b, i: (b, 0, 0)),
                  _resident((D_SSD, D_MODEL)), _resident((D_Q, D_MODEL)), _resident((D_MODEL, D_MODEL))],
        out_specs=pl.BlockSpec((1, tm, D_MODEL), row_map),
        out_shape=jax.ShapeDtypeStruct((bsz, n_rows, D_MODEL), F32),
        compiler_params=pltpu.CompilerParams(dimension_semantics=("arbitrary", "arbitrary"),
                                             vmem_limit_bytes=_vmem_limit(48 << 20)),
        name="merge",
    )(yn, att, gates, x, modv, w_os, w_oa, w_out)


FF_CHUNK = 256


def _ffn_kernel(x_ref, xp_ref, xn_ref, mod_ref, g_ref, wup_ref, cw_ref, cb_ref, wdn_ref, fg_ref,
                o_ref, ua_scr, ub_scr, act_scr, *, tm):
    i = pl.program_id(1)
    n_i = pl.num_programs(1)
    g = g_ref[...]
    shift = mod_ref[0, :, 3 * D_MODEL:4 * D_MODEL]
    scale = mod_ref[0, :, 4 * D_MODEL:5 * D_MODEL]
    gt2 = mod_ref[0, :, 5 * D_MODEL:6 * D_MODEL]
    x = x_ref[0]
    h = _modulated_norm(x, g, shift, scale)
    hp = _modulated_norm(xp_ref[0], g, shift, scale) * (i > 0).astype(F32)
    hn = _modulated_norm(xn_ref[0], g, shift, scale) * (i < n_i - 1).astype(F32)
    h_ext = jnp.concatenate([h, hn, hp], axis=0).astype(BF16)
    for c in range(D_FF // FF_CHUNK):
        c0 = c * FF_CHUNK
        ua_scr[c] = _dot(h_ext, wup_ref[:, c0:c0 + FF_CHUNK])
        ub_scr[c] = _dot(h_ext, wup_ref[:, D_FF + c0:D_FF + c0 + FF_CHUNK])
    for c in range(D_FF // FF_CHUNK):
        c0 = c * FF_CHUNK
        b0 = D_FF + c0
        a = _conv3_rows(ua_scr[c], tm, cw_ref[:, c0:c0 + FF_CHUNK], cb_ref[:, c0:c0 + FF_CHUNK])
        b = _conv3_rows(ub_scr[c], tm, cw_ref[:, b0:b0 + FF_CHUNK], cb_ref[:, b0:b0 + FF_CHUNK])
        act_scr[:, c0:c0 + FF_CHUNK] = (_silu(a) * b).astype(BF16)
    f = _dot(act_scr[...], wdn_ref[...])
    x2 = x + gt2 * f
    ms = jnp.mean(x2 * x2, axis=-1, keepdims=True)
    o_ref[0] = x2 * lax.rsqrt(ms + EPS) * fg_ref[...]


def _ffn_call(x1, modv, norm_g, w_up, conv_w, conv_b, w_down, final_g, tm):
    bsz, n_rows, _ = x1.shape
    prev_map, next_map = _halo_maps(tm, n_rows)
    row_map = lambda b, i: (b, i, 0)
    return pl.pallas_call(
        functools.partial(_ffn_kernel, tm=tm),
        grid=(bsz, n_rows // tm),
        in_specs=[pl.BlockSpec((1, tm, D_MODEL), row_map),
                  pl.BlockSpec((1, HALO, D_MODEL), prev_map),
                  pl.BlockSpec((1, HALO, D_MODEL), next_map),
                  pl.BlockSpec((1, 1, N_MOD * D_MODEL), lambda b, i: (b, 0, 0)),
                  _resident((1, D_MODEL)),
                  _resident((D_MODEL, 2 * D_FF)), _resident((3, 2 * D_FF)), _resident((1, 2 * D_FF)),
                  _resident((D_FF, D_MODEL)), _resident((1, D_MODEL))],
        out_specs=pl.BlockSpec((1, tm, D_MODEL), row_map),
        out_shape=jax.ShapeDtypeStruct((bsz, n_rows, D_MODEL), F32),
        scratch_shapes=[pltpu.VMEM((D_FF // FF_CHUNK, tm + 2 * HALO, FF_CHUNK), F32),
                        pltpu.VMEM((D_FF // FF_CHUNK, tm + 2 * HALO, FF_CHUNK), F32),
                        pltpu.VMEM((tm, D_FF), BF16)],
        compiler_params=pltpu.CompilerParams(dimension_semantics=("arbitrary", "arbitrary"),
                                             vmem_limit_bytes=_vmem_limit(48 << 20)),
        name="ffn",
    )(x1, x1, x1, modv, norm_g, w_up, conv_w, conv_b, w_down, final_g)


def _rope_tables(n_rows):
    half = ATT_HEADDIM // 2
    inv = (ROPE_BASE ** (-np.arange(0, half, 2, dtype=np.float32) / half)).astype(np.float32)
    pos = np.arange(n_rows)
    rowp = (pos // GRID_W).astype(np.float32)[:, None] * inv[None]
    colp = (pos % GRID_W).astype(np.float32)[:, None] * inv[None]
    cos = np.concatenate([np.cos(rowp), np.cos(rowp), np.cos(colp), np.cos(colp)], axis=-1)
    sin = np.concatenate([-np.sin(rowp), np.sin(rowp), -np.sin(colp), np.sin(colp)], axis=-1)
    reps = LANES // ATT_HEADDIM
    return (jnp.asarray(np.tile(cos, (1, reps)), dtype=F32), jnp.asarray(np.tile(sin, (1, reps)), dtype=F32))


def kernel(x, c, ctx, c_ctx, w_mod, b_mod, norm1_g, norm2_g, w_in, ssd_conv_w, ssd_conv_b, ssd_dt_bias,
           ssd_a_log, ssd_d, ssd_norm_g, w_o_ssd, w_o_att, att_sink, w_out, w_up, ffn_conv_w, ffn_conv_b,
           w_down, final_g):
    depth = w_mod.shape[0]
    assert depth == 1, "single-layer block"
    l = 0
    bsz, n_lat, _ = x.shape
    assert bsz + 1 <= SUBLANES

    cc = jnp.concatenate([c, c_ctx[None], jnp.zeros((SUBLANES - bsz - 1, D_MODEL), F32)], axis=0)
    mod_all = _mod_call(cc, w_mod[l], b_mod[l][None])
    mod_lat = mod_all[:bsz, None, :]
    mod_ctx = mod_all[bsz:bsz + 1, None, :]

    wi = w_in[l].astype(BF16)
    w_q = (wi[:, Q0:K0].reshape(D_MODEL, ATT_KV_HEADS, ATT_GROUP, ATT_HEADDIM)
           .transpose(0, 2, 1, 3).reshape(D_MODEL, D_Q))
    w_qkvg = jnp.concatenate([w_q, wi[:, K0:D_IN]], axis=1)
    dtb = ssd_dt_bias[l].reshape(1, 2 * SSD_HEADS)
    a_mult = -jnp.exp(ssd_a_log[l].reshape(1, 2 * SSD_HEADS))
    consts = {"conv_w": ssd_conv_w[l], "conv_b": ssd_conv_b[l][None],
              "dt_bias": jnp.concatenate([dtb, dtb], axis=1),
              "a_mult": jnp.concatenate([jnp.ones_like(a_mult), a_mult], axis=1)}
    rope = _rope_tables(n_lat)
    g1 = norm1_g[l][None]

    xbc_c, dtla_c, k_c, v_c = _inproj_call(ctx, mod_ctx, False, g1, wi, w_qkvg, consts, None,
                                           ctx.shape[1], False)
    xbc_l, dtla_l, k_l, v_l, zg, q_l, gates = _inproj_call(x, mod_lat, True, g1, wi, w_qkvg, consts, rope,
                                                           INPROJ_ROWS, True)

    (hf_c,) = _ssd_call(xbc_c, dtla_c, None, False, False)
    (hb_c,) = _ssd_call(xbc_c, dtla_c, None, True, False)
    yf, _ = _ssd_call(xbc_l, dtla_l, hf_c, False, True)
    dskip = jnp.repeat(ssd_d[l], SSD_HEADDIM)[None]
    yn, _ = _ssd_call(xbc_l, dtla_l, hb_c, True, True, fuse=(yf, zg, dskip, ssd_norm_g[l][None]))

    sink = att_sink[l].astype(F32)
    att = _attn_call(q_l, k_l, v_l, k_c, v_c, sink)

    w_oa = (w_o_att[l].reshape(ATT_KV_HEADS, ATT_GROUP, ATT_HEADDIM, D_MODEL)
            .transpose(1, 0, 2, 3).reshape(D_Q, D_MODEL))
    x1 = _merge_call(yn, att, gates, x, mod_lat, w_o_ssd[l].astype(BF16),
                     w_oa.astype(BF16), w_out[l].astype(BF16), MERGE_ROWS)
    return _ffn_call(x1, mod_lat, norm2_g[l][None], w_up[l].astype(BF16), ffn_conv_w[l],
                     ffn_conv_b[l][None], w_down[l].astype(BF16), final_g[None], FFN_ROWS)
```
